```python
import jax, jax.numpy as jnp
from jax import lax
import numpy as np

D_MODEL = 1024
BATCH = 16
SEQ = 2048
DEPTH = 4

CTX_LEN = 256
GRID_W = 64
EPS = 1e-6
ROPE_BASE = 10000.0
Q_BLOCK = 128

MLA_HEADS = 8
MLA_Q_RANK = 384
MLA_KV_RANK = 256
MLA_NOPE = 64
MLA_ROPE = 32
MLA_V = 64
MLA_SCALE = (MLA_NOPE + MLA_ROPE) ** -0.5

GLA_HEADS = 4
GLA_DK = 64
GLA_DV = 128
GLA_GATE_RANK = 16
GLA_GATE_TAU = 16.0
GLA_CHUNK = 64
GLA_QK_W = GLA_HEADS * GLA_DK
GLA_V_W = GLA_HEADS * GLA_DV

NA_HEADS = 8
NA_HEAD_DIM = 64
NA_KH = 8
NA_KW = 16
NA_W = NA_HEADS * NA_HEAD_DIM
NA_SCALE = NA_HEAD_DIM ** -0.5

FFN_HIDDEN = ((8 * D_MODEL + 3 * 256 - 1) // (3 * 256)) * 256

N_BRANCH = 3
MLA_OUT_W = MLA_HEADS * MLA_V
A_SPLITS = (MLA_Q_RANK, MLA_KV_RANK, MLA_ROPE)
B_SPLITS = (GLA_QK_W, GLA_QK_W, GLA_V_W, GLA_V_W, GLA_GATE_RANK, GLA_GATE_RANK)
C_SPLITS = (NA_W, NA_W, NA_W)
GROUP_SPLITS = (sum(A_SPLITS), sum(B_SPLITS), sum(C_SPLITS), N_BRANCH * D_MODEL)
IN_WIDTH = sum(GROUP_SPLITS)

kernel_name = "hybrid_mla_gla_natten_adaln_trunk"


def split_cols(z, sizes):
    idx = np.cumsum(sizes)[:-1].tolist()
    return jnp.split(z, idx, axis=-1)


def rms_norm(x, w):
    xf = x.astype(jnp.float32)
    y = xf * lax.rsqrt(jnp.mean(xf * xf, axis=-1, keepdims=True) + EPS)
    return (y * w.astype(jnp.float32)).astype(x.dtype)


def modulate(h, shift, scale):
    return h * (1 + scale) + shift


def to_heads(t, n_heads):
    b, l, w = t.shape
    return t.reshape(b, l, n_heads, w // n_heads).transpose(0, 2, 1, 3)


def from_heads(t):
    b, h, l, d = t.shape
    return t.transpose(0, 2, 1, 3).reshape(b, l, h * d)


def axial_rope_tables(length):
    t = jnp.arange(length)
    rows = (t // GRID_W).astype(jnp.float32)
    cols = (t % GRID_W).astype(jnp.float32)
    n_freq = MLA_ROPE // 4
    inv_freq = ROPE_BASE ** (-jnp.arange(n_freq, dtype=jnp.float32) / n_freq)
    ang = jnp.concatenate([rows[:, None] * inv_freq, cols[:, None] * inv_freq], axis=-1)
    return jnp.cos(ang), jnp.sin(ang)


def apply_rope(x, cos, sin):
    half = x.shape[-1] // 2
    x1, x2 = x[..., :half], x[..., half:]
    cos = cos.astype(x.dtype)
    sin = sin.astype(x.dtype)
    return jnp.concatenate([x1 * cos - x2 * sin, x1 * sin + x2 * cos], axis=-1)


def dense_attend(q, k, v):
    s = jnp.einsum('bhqd,bhkd->bhqk', q, k).astype(jnp.float32)
    p = jax.nn.softmax(s, axis=-1).astype(v.dtype)
    return jnp.einsum('bhqk,bhkd->bhqd', p, v)


def mla_queries(a, q_norm_w, w_q_up):
    q_down = a[..., :MLA_Q_RANK]
    q = to_heads(rms_norm(q_down, q_norm_w) @ w_q_up, MLA_HEADS)
    return q[..., :MLA_NOPE], q[..., MLA_NOPE:]


def mla_keys(a, kv_norm_w, w_kv_up):
    _, kv_down, k_rope = split_cols(a, A_SPLITS)
    kv = to_heads(rms_norm(kv_down, kv_norm_w) @ w_kv_up, MLA_HEADS)
    return kv[..., :MLA_NOPE], k_rope, kv[..., MLA_NOPE:]


def mla_attend(q_nope, q_rope, k_nope, k_rope, v):
    s = (jnp.einsum('bhqd,bhkd->bhqk', q_nope, k_nope)
         + jnp.einsum('bhqr,bkr->bhqk', q_rope, k_rope)).astype(jnp.float32) * MLA_SCALE
    p = jax.nn.softmax(s, axis=-1).astype(v.dtype)
    return jnp.einsum('bhqk,bhkd->bhqd', p, v)


def mla_mixer(a_lat, a_ctx, q_norm_w, kv_norm_w, w_q_up, w_kv_up, cos, sin, need_ctx_out):
    qn, qr = mla_queries(a_lat, q_norm_w, w_q_up)
    qr = apply_rope(qr, cos, sin)
    kn, kr, v = mla_keys(a_lat, kv_norm_w, w_kv_up)
    kr = apply_rope(kr, cos, sin)
    kn_c, kr_c, v_c = mla_keys(a_ctx, kv_norm_w, w_kv_up)
    kn_all = jnp.concatenate([kn, kn_c], axis=2)
    kr_all = jnp.concatenate([kr, kr_c], axis=1)
    v_all = jnp.concatenate([v, v_c], axis=2)
    b, h, l, _ = qn.shape
    nb = l // Q_BLOCK

    def blocks(t):
        return t.reshape(b, h, nb, Q_BLOCK, t.shape[-1]).transpose(2, 0, 1, 3, 4)

    o = lax.map(lambda qb: mla_attend(qb[0], qb[1], kn_all, kr_all, v_all), (blocks(qn), blocks(qr)))
    o = o.transpose(1, 2, 0, 3, 4).reshape(b, h, l, MLA_V)
    y_lat = from_heads(o)
    y_ctx = None
    if need_ctx_out:
        qn_c, qr_c = mla_queries(a_ctx, q_norm_w, w_q_up)
        y_ctx = from_heads(mla_attend(qn_c, qr_c, kn_c, kr_c, v_c))
    return y_lat, y_ctx


def gla_chunked(q, k, v, log_a, s0):
    b, h, l, _ = q.shape
    dv = v.shape[-1]
    n = l // GLA_CHUNK

    def ch(t):
        return t.reshape(b, h, n, GLA_CHUNK, t.shape[-1])

    q, k, v, log_a = ch(q), ch(k), ch(v), ch(log_a)
    cum = jnp.cumsum(log_a, axis=3)
    last = cum[:, :, :, -1:, :]
    q_dec = q * jnp.exp(cum)
    k_inv = k * jnp.exp(-cum)
    k_end = k * jnp.exp(last - cum)
    lower = jnp.tril(jnp.ones((GLA_CHUNK, GLA_CHUNK), dtype=bool))
    att = jnp.where(lower, jnp.einsum('bhncd,bhnsd->bhncs', q_dec, k_inv), 0.0)
    o_intra = jnp.einsum('bhncs,bhnse->bhnce', att, v)
    kv_chunk = jnp.einsum('bhncd,bhnce->bhnde', k_end, v)
    decay = jnp.exp(last[:, :, :, 0, :])

    def step(state, inp):
        dec, kv = inp
        return dec[..., None] * state + kv, state

    s_final, s_prev = lax.scan(step, s0, (jnp.moveaxis(decay, 2, 0), jnp.moveaxis(kv_chunk, 2, 0)))
    s_prev = jnp.moveaxis(s_prev, 0, 2)
    o_inter = jnp.einsum('bhncd,bhnde->bhnce', q_dec, s_prev)
    return (o_intra + o_inter).reshape(b, h, l, dv), s_final


def gla_bidir(q, k, v, la_f, la_b, s0_f, s0_b):
    o_f, s_f = gla_chunked(q, k, v, la_f, s0_f)
    flip = lambda t: jnp.flip(t, axis=2)
    o_b, s_b = gla_chunked(flip(q), flip(k), flip(v), flip(la_b), s0_b)
    return o_f + flip(o_b), s_f, s_b


def gla_log_decay(lr, w_up, b_up):
    z = (lr @ w_up + b_up).astype(jnp.float32)
    return to_heads(jax.nn.log_sigmoid(z) / GLA_GATE_TAU, GLA_HEADS)


def gla_mixer(b_lat, b_ctx, w_gate_f, b_gate_f, w_gate_b, b_gate_b, norm_w, need_ctx_out):
    def prep(t):
        q, k, v, g, lr_f, lr_b = split_cols(t, B_SPLITS)
        q = to_heads(q, GLA_HEADS).astype(jnp.float32) * (GLA_DK ** -0.5)
        k = to_heads(k, GLA_HEADS).astype(jnp.float32)
        v = to_heads(v, GLA_HEADS).astype(jnp.float32)
        return q, k, v, g, gla_log_decay(lr_f, w_gate_f, b_gate_f), gla_log_decay(lr_b, w_gate_b, b_gate_b)

    def finish(o, g):
        o = rms_norm(o.transpose(0, 2, 1, 3), norm_w)
        b, l = o.shape[0], o.shape[1]
        return o.reshape(b, l, GLA_V_W).astype(g.dtype) * jax.nn.silu(g)

    q_c, k_c, v_c, g_c, laf_c, lab_c = prep(b_ctx)
    s0 = jnp.zeros((q_c.shape[0], GLA_HEADS, GLA_DK, GLA_DV), jnp.float32)
    o_c, s_f, s_b = gla_bidir(q_c, k_c, v_c, laf_c, lab_c, s0, s0)
    q, k, v, g, laf, lab = prep(b_lat)
    o_l, _, _ = gla_bidir(q, k, v, laf, lab, s_f, s_b)
    y_ctx = finish(o_c, g_c) if need_ctx_out else None
    return finish(o_l, g), y_ctx


def na_mixer(c_lat, c_ctx, rpb, need_ctx_out):
    q, k, v = [to_heads(t, NA_HEADS) for t in split_cols(c_lat, C_SPLITS)]
    q_c, k_c, v_c = [to_heads(t, NA_HEADS) for t in split_cols(c_ctx, C_SPLITS)]
    b, h, l, d = q.shape
    rows = l // GRID_W
    kh = min(NA_KH, rows)
    grid = lambda t: t.reshape(b, h, rows, GRID_W, d)
    qg, kg, vg = grid(q * NA_SCALE), grid(k), grid(v)
    col = jnp.arange(GRID_W)
    col_start = jnp.clip(col - NA_KW // 2, 0, GRID_W - NA_KW)
    col_mask = (col[None, :] >= col_start[:, None]) & (col[None, :] < col_start[:, None] + NA_KW)
    col_off = jnp.clip(col[None, :] - col[:, None] + NA_KW - 1, 0, 2 * NA_KW - 2)
    band = kh * GRID_W

    def row_attend(r):
        r0 = jnp.clip(r - kh // 2, 0, rows - kh)
        kb = lax.dynamic_slice_in_dim(kg, r0, kh, axis=2)
        vb = lax.dynamic_slice_in_dim(vg, r0, kh, axis=2)
        qr = lax.dynamic_index_in_dim(qg, r, axis=2, keepdims=False)
        row_off = r0 + jnp.arange(kh) - r + NA_KH - 1
        bias = rpb[:, row_off[:, None, None], col_off[None, :, :]].transpose(0, 2, 1, 3)
        s_band = jnp.einsum('bhqd,bhrkd->bhqrk', qr, kb).astype(jnp.float32) + bias.astype(jnp.float32)
        s_band = jnp.where(col_mask[:, None, :], s_band, -jnp.inf)
        s_ctx = jnp.einsum('bhqd,bhkd->bhqk', qr, k_c).astype(jnp.float32)
        s = jnp.concatenate([s_band.reshape(b, h, GRID_W, band), s_ctx], axis=-1)
        p = jax.nn.softmax(s, axis=-1).astype(v.dtype)
        p_band = p[..., :band].reshape(b, h, GRID_W, kh, GRID_W)
        return (jnp.einsum('bhqrk,bhrkd->bhqd', p_band, vb)
                + jnp.einsum('bhqk,bhkd->bhqd', p[..., band:], v_c))

    o = lax.map(row_attend, jnp.arange(rows))
    y_lat = o.transpose(1, 0, 3, 2, 4).reshape(b, l, h * d)
    y_ctx = from_heads(dense_attend(q_c * NA_SCALE, k_c, v_c)) if need_ctx_out else None
    return y_lat, y_ctx


def merge_branches(ya, yb, yc, gates, w_a_o, w_b_o, w_c_o, w_out):
    ga, gb, gc = jnp.split(jax.nn.sigmoid(gates), N_BRANCH, axis=-1)
    return (ga * (ya @ w_a_o) + gb * (yb @ w_b_o) + gc * (yc @ w_c_o)) @ w_out


def swiglu(u, w_ffn_in, w_ffn_out):
    gate, up = jnp.split(u @ w_ffn_in, 2, axis=-1)
    return (jax.nn.silu(gate) * up) @ w_ffn_out


def hybrid_layer(h, hc, mod, modc, norm1_w, w_in, mla_q_norm_w, mla_kv_norm_w, mla_w_q_up, mla_w_kv_up,
                 gla_w_gate_f, gla_b_gate_f, gla_w_gate_b, gla_b_gate_b, gla_norm_w, na_rpb,
                 w_a_o, w_b_o, w_c_o, w_out, norm2_w, w_ffn_in, w_ffn_out, cos, sin, need_ctx_out):
    sh1, sc1, g1, sh2, sc2, g2 = mod
    csh1, csc1, cg1, csh2, csc2, cg2 = modc
    z = modulate(rms_norm(h, norm1_w), sh1, sc1) @ w_in
    zc = modulate(rms_norm(hc, norm1_w), csh1, csc1) @ w_in
    a, bb, cc, gates = split_cols(z, GROUP_SPLITS)
    a_c, bb_c, cc_c, gates_c = split_cols(zc, GROUP_SPLITS)
    ya, ya_c = mla_mixer(a, a_c, mla_q_norm_w, mla_kv_norm_w, mla_w_q_up, mla_w_kv_up, cos, sin, need_ctx_out)
    yb, yb_c = gla_mixer(bb, bb_c, gla_w_gate_f, gla_b_gate_f, gla_w_gate_b, gla_b_gate_b, gla_norm_w, need_ctx_out)
    yc, yc_c = na_mixer(cc, cc_c, na_rpb, need_ctx_out)
    h = h + g1 * merge_branches(ya, yb, yc, gates, w_a_o, w_b_o, w_c_o, w_out)
    h = h + g2 * swiglu(modulate(rms_norm(h, norm2_w), sh2, sc2), w_ffn_in, w_ffn_out)
    if need_ctx_out:
        hc = hc + cg1 * merge_branches(ya_c, yb_c, yc_c, gates_c, w_a_o, w_b_o, w_c_o, w_out)
        hc = hc + cg2 * swiglu(modulate(rms_norm(hc, norm2_w), csh2, csc2), w_ffn_in, w_ffn_out)
    return h, hc


def setup_inputs(seed: int = 0) -> dict:
    key = jax.random.key(seed)
    ks = iter(jax.random.split(key, 32))

    def nrm(shape, scale):
        return jax.random.normal(next(ks), shape, jnp.float32) * scale

    L, D = DEPTH, D_MODEL
    return {
        "x": nrm((BATCH, SEQ, D), 1.0),
        "c": nrm((BATCH, D), 1.0),
        "ctx": nrm((BATCH, CTX_LEN, D), 1.0),
        "c_ctx": nrm((D,), 1.0),
        "w_mod": nrm((L, D, 6 * D), 0.5 * D ** -0.5),
        "b_mod": nrm((L, 6 * D), 0.02),
        "norm1_w": 1.0 + nrm((L, D), 0.02),
        "w_in": nrm((L, D, IN_WIDTH), D ** -0.5),
        "mla_q_norm_w": 1.0 + nrm((L, MLA_Q_RANK), 0.02),
        "mla_kv_norm_w": 1.0 + nrm((L, MLA_KV_RANK), 0.02),
        "mla_w_q_up": nrm((L, MLA_Q_RANK, MLA_HEADS * (MLA_NOPE + MLA_ROPE)), MLA_Q_RANK ** -0.5),
        "mla_w_kv_up": nrm((L, MLA_KV_RANK, MLA_HEADS * (MLA_NOPE + MLA_V)), MLA_KV_RANK ** -0.5),
        "gla_w_gate_f": nrm((L, GLA_GATE_RANK, GLA_QK_W), GLA_GATE_RANK ** -0.5),
        "gla_b_gate_f": nrm((L, GLA_QK_W), 0.1),
        "gla_w_gate_b": nrm((L, GLA_GATE_RANK, GLA_QK_W), GLA_GATE_RANK ** -0.5),
        "gla_b_gate_b": nrm((L, GLA_QK_W), 0.1),
        "gla_norm_w": 1.0 + nrm((L, GLA_DV), 0.02),
        "na_rpb": nrm((L, NA_HEADS, 2 * NA_KH - 1, 2 * NA_KW - 1), 0.1),
        "w_a_o": nrm((L, MLA_OUT_W, D), MLA_OUT_W ** -0.5),
        "w_b_o": nrm((L, GLA_V_W, D), GLA_V_W ** -0.5),
        "w_c_o": nrm((L, NA_W, D), NA_W ** -0.5),
        "w_out": nrm((L, D, D), D ** -0.5),
        "norm2_w": 1.0 + nrm((L, D), 0.02),
        "w_ffn_in": nrm((L, D, 2 * FFN_HIDDEN), D ** -0.5),
        "w_ffn_out": nrm((L, FFN_HIDDEN, D), FFN_HIDDEN ** -0.5),
        "final_norm_w": 1.0 + nrm((D,), 0.02),
    }


def reference(x, c, ctx, c_ctx, w_mod, b_mod, norm1_w, w_in, mla_q_norm_w, mla_kv_norm_w, mla_w_q_up,
              mla_w_kv_up, gla_w_gate_f, gla_b_gate_f, gla_w_gate_b, gla_b_gate_b, gla_norm_w, na_rpb,
              w_a_o, w_b_o, w_c_o, w_out, norm2_w, w_ffn_in, w_ffn_out, final_norm_w):
    cos, sin = axial_rope_tables(x.shape[1])
    c_act = jax.nn.silu(c)[:, None, :]
    cc_act = jax.nn.silu(c_ctx)
    h, hc = x, ctx
    for i in range(DEPTH):
        mod = jnp.split(c_act @ w_mod[i] + b_mod[i], 6, axis=-1)
        modc = jnp.split(cc_act @ w_mod[i] + b_mod[i], 6, axis=-1)
        h, hc = hybrid_layer(h, hc, mod, modc, norm1_w[i], w_in[i], mla_q_norm_w[i], mla_kv_norm_w[i],
                             mla_w_q_up[i], mla_w_kv_up[i], gla_w_gate_f[i], gla_b_gate_f[i],
                             gla_w_gate_b[i], gla_b_gate_b[i], gla_norm_w[i], na_rpb[i],
                             w_a_o[i], w_b_o[i], w_c_o[i], w_out[i], norm2_w[i], w_ffn_in[i],
                             w_ffn_out[i], cos, sin, i < DEPTH - 1)
    return rms_norm(h, final_norm_w)
```

```python
import functools

import jax
import jax.numpy as jnp
import numpy as np
from jax import lax
from jax.experimental import pallas as pl
from jax.experimental.pallas import tpu as pltpu

F32 = jnp.float32
BF16 = jnp.bfloat16

D_MODEL = 1024
SEQ = 2048
CTX_LEN = 256
GRID_W = 64
GRID_ROWS = SEQ // GRID_W
EPS = 1e-6
ROPE_BASE = 10000.0

MLA_HEADS = 8
MLA_Q_RANK = 384
MLA_KV_RANK = 256
MLA_NOPE = 64
MLA_ROPE = 32
MLA_V = 64
MLA_SCALE = (MLA_NOPE + MLA_ROPE) ** -0.5

GLA_HEADS = 4
GLA_DK = 64
GLA_DV = 128
GLA_GATE_RANK = 16
GLA_GATE_TAU = 16.0
GLA_CHUNK = 64
GLA_QK_W = GLA_HEADS * GLA_DK
GLA_V_W = GLA_HEADS * GLA_DV

NA_HEADS = 8
NA_HEAD_DIM = 64
NA_KH = 8
NA_KW = 16
NA_W = NA_HEADS * NA_HEAD_DIM
NA_SCALE = NA_HEAD_DIM ** -0.5
NA_QROWS = 4
NA_BAND_ROWS = NA_KH + NA_QROWS - 1
NA_BAND = NA_BAND_ROWS * GRID_W
NA_QBLK = NA_QROWS * GRID_W

FFN_HIDDEN = 2816
LANE = 128

Z_AQ = 0
Z_LR = 384
Z_CQ = 512
Z_CK = 1024
Z_CV = 1536
Z_BV = 2048
Z_BG = 2560
Z_BQ = 3072
Z_BK = 3328
Z_AKV = 3584
Z_AKR = 3840
Z_GATES = 4096
Z_WIDTH = 7168

TM_IN = 1024
TN_IN = 1024
TM_TOK = 512
TQ = 256
VMEM_LIMIT = 56 * 1024 * 1024


def _cparams(sem):
    return pltpu.CompilerParams(dimension_semantics=sem, vmem_limit_bytes=VMEM_LIMIT)


def _dot(a, b):
    return jnp.dot(a, b, preferred_element_type=F32)


def _dot_nt(a, b):
    return lax.dot_general(a, b, (((1,), (1,)), ((), ())), preferred_element_type=F32)


def _dot_tn(a, b):
    return lax.dot_general(a, b, (((0,), (0,)), ((), ())), preferred_element_type=F32)


def _sigmoid(x):
    return 1.0 / (1.0 + jnp.exp(-x))


def _rms(x, w):
    return x * lax.rsqrt(jnp.mean(x * x, axis=-1, keepdims=True) + EPS) * w


def _split3(x):
    hi = x.astype(BF16)
    r1 = x - hi.astype(F32)
    mid = r1.astype(BF16)
    lo = (r1 - mid.astype(F32)).astype(BF16)
    return hi, mid, lo


def _mod_kernel(c_ref, w_ref, b_ref, o_ref):
    c = c_ref[...]
    a = c * _sigmoid(c)
    a_hi = a.astype(BF16)
    a_lo = (a - a_hi.astype(F32)).astype(BF16)
    w = w_ref[0]
    w_hi = w.astype(BF16)
    w_lo = (w - w_hi.astype(F32)).astype(BF16)
    o_ref[0] = _dot(a_hi, w_hi) + _dot(a_lo, w_hi) + _dot(a_hi, w_lo) + b_ref[0]


def _modulation(c_all, w_mod, b_mod):
    depth, d, n = w_mod.shape
    rows = c_all.shape[0]
    tn = 1536
    return pl.pallas_call(
        _mod_kernel,
        name="adaln_mod",
        grid=(depth, n // tn),
        in_specs=[
            pl.BlockSpec((rows, d), lambda l, j: (0, 0)),
            pl.BlockSpec((1, d, tn), lambda l, j: (l, 0, j)),
            pl.BlockSpec((1, 1, tn), lambda l, j: (l, 0, j)),
        ],
        out_specs=pl.BlockSpec((1, rows, tn), lambda l, j: (l, 0, j)),
        out_shape=jax.ShapeDtypeStruct((depth, rows, n), F32),
        compiler_params=_cparams(("parallel", "parallel")),
    )(c_all, w_mod, b_mod.reshape(depth, 1, n))


def _inproj_kernel(h_ref, mod_ref, nw_ref, w_ref, o_ref, xn_ref):
    @pl.when(pl.program_id(1) == 0)
    def _():
        y = _rms(h_ref[...], nw_ref[...])
        xn_ref[...] = (y * (1.0 + mod_ref[0, 1:2, :]) + mod_ref[0, 0:1, :]).astype(BF16)

    o_ref[...] = _dot(xn_ref[...], w_ref[...]).astype(BF16)


def _inproj(h, mod, norm_w, w_in_p, n_lat_tiles, ctx_row):
    ntok, d = h.shape
    per_batch = SEQ // TM_IN
    return pl.pallas_call(
        _inproj_kernel,
        name="in_proj",
        grid=(ntok // TM_IN, Z_WIDTH // TN_IN),
        in_specs=[
            pl.BlockSpec((TM_IN, d), lambda i, j: (i, 0)),
            pl.BlockSpec((1, 6, d), lambda i, j: (jnp.where(i < n_lat_tiles, i // per_batch, ctx_row), 0, 0)),
            pl.BlockSpec((1, d), lambda i, j: (0, 0)),
            pl.BlockSpec((d, TN_IN), lambda i, j: (0, j)),
        ],
        out_specs=pl.BlockSpec((TM_IN, TN_IN), lambda i, j: (i, j)),
        out_shape=jax.ShapeDtypeStruct((ntok, Z_WIDTH), BF16),
        scratch_shapes=[pltpu.VMEM((TM_IN, d), BF16)],
        compiler_params=_cparams(("parallel", "arbitrary")),
    )(h, mod, norm_w, w_in_p)


def _mla_prep_kernel(aq_ref, akv_ref, akr_ref, qnw_ref, kvnw_ref, wq_ref, wkn_ref, wv_ref,
                     tqc_ref, tqs_ref, tkc_ref, tks_ref, q_out, k_out, v_out):
    qn = _rms(aq_ref[...].astype(F32), qnw_ref[...]).astype(BF16)
    qq = _dot(qn, wq_ref[...])
    hw = MLA_HEADS * LANE
    tqc = tqc_ref[...]
    tqs = tqs_ref[...]
    for h in range(MLA_HEADS):
        lo = h * LANE
        q_out[:, lo:lo + LANE] = (qq[:, lo:lo + LANE] * tqc + qq[:, hw + lo:hw + lo + LANE] * tqs).astype(BF16)
    kvn = _rms(akv_ref[...].astype(F32), kvnw_ref[...]).astype(BF16)
    kn = _dot(kvn, wkn_ref[...])
    r = akr_ref[...].astype(F32)
    kr = r[:, :LANE] * tkc_ref[...] + r[:, LANE:] * tks_ref[...]
    for h in range(MLA_HEADS):
        lo = h * LANE
        k_out[:, lo:lo + LANE] = (kn[:, lo:lo + LANE] + kr).astype(BF16)
    v_out[...] = _dot(kvn, wv_ref[...]).astype(BF16)


def _mla_prep(z, qnw, kvnw, wq, wkn, wv, tabs, n_lat_tiles):
    ntok = z.shape[0]
    tm = TM_IN
    per_batch = SEQ // tm
    tab_idx = lambda i: (jnp.where(i < n_lat_tiles, i % per_batch, per_batch), 0)
    const = lambda i: (0, 0)
    hw = MLA_HEADS * LANE
    tab_spec = pl.BlockSpec((tm, LANE), tab_idx)
    return pl.pallas_call(
        _mla_prep_kernel,
        name="mla_prep",
        grid=(ntok // tm,),
        in_specs=[
            pl.BlockSpec((tm, MLA_Q_RANK), lambda i: (i, Z_AQ // MLA_Q_RANK)),
            pl.BlockSpec((tm, MLA_KV_RANK), lambda i: (i, Z_AKV // MLA_KV_RANK)),
            pl.BlockSpec((tm, 2 * LANE), lambda i: (i, Z_AKR // (2 * LANE))),
            pl.BlockSpec((1, MLA_Q_RANK), const),
            pl.BlockSpec((1, MLA_KV_RANK), const),
            pl.BlockSpec((MLA_Q_RANK, 2 * hw), const),
            pl.BlockSpec((MLA_KV_RANK, hw), const),
            pl.BlockSpec((MLA_KV_RANK, MLA_HEADS * MLA_V), const),
            tab_spec, tab_spec, tab_spec, tab_spec,
        ],
        out_specs=[
            pl.BlockSpec((tm, hw), lambda i: (i, 0)),
            pl.BlockSpec((tm, hw), lambda i: (i, 0)),
            pl.BlockSpec((tm, MLA_HEADS * MLA_V), lambda i: (i, 0)),
        ],
        out_shape=[
            jax.ShapeDtypeStruct((ntok, hw), BF16),
            jax.ShapeDtypeStruct((ntok, hw), BF16),
            jax.ShapeDtypeStruct((ntok, MLA_HEADS * MLA_V), BF16),
        ],
        compiler_params=_cparams(("parallel",)),
    )(z, z, z, qnw, kvnw, wq, wkn, wv, *tabs)


def _softmax_pv(s_parts, v_parts):
    m = None
    for s in s_parts:
        mi = jnp.max(s, axis=-1, keepdims=True)
        m = mi if m is None else jnp.maximum(m, mi)
    l = None
    o = None
    for s, v in zip(s_parts, v_parts):
        p = jnp.exp(s - m)
        li = jnp.sum(p, axis=-1, keepdims=True)
        oi = _dot(p.astype(BF16), v)
        l = li if l is None else l + li
        o = oi if o is None else o + oi
    return o / l


def _mla_attn_kernel(q_ref, kl_ref, kc_ref, vl_ref, vc_ref, o_ref, *, n_lat_q):
    lane = lax.broadcasted_iota(jnp.int32, (TQ, LANE), 1)

    def attend(with_lat):
        for p in range(MLA_HEADS // 2):
            vc = vc_ref[:, p * LANE:(p + 1) * LANE]
            vl = vl_ref[:, p * LANE:(p + 1) * LANE] if with_lat else None
            outs = []
            for hh in range(2):
                lo = (2 * p + hh) * LANE
                q = q_ref[:, lo:lo + LANE]
                s_parts = [_dot_nt(q, kc_ref[:, lo:lo + LANE])]
                v_parts = [vc]
                if with_lat:
                    s_parts.append(_dot_nt(q, kl_ref[:, lo:lo + LANE]))
                    v_parts.append(vl)
                outs.append(_softmax_pv(s_parts, v_parts))
            o_ref[:, p * LANE:(p + 1) * LANE] = jnp.where(lane < MLA_V, outs[0], outs[1]).astype(BF16)

    t = pl.program_id(1)
    pl.when(t < n_lat_q)(lambda: attend(True))
    pl.when(t >= n_lat_q)(lambda: attend(False))


def _q_row_block(n_lat_q, ctx_base):
    return lambda b, t: jnp.where(t < n_lat_q, b * n_lat_q + t, ctx_base + b)


def _mla_attn(q, k, v, batch, need_ctx):
    ntok = q.shape[0]
    n_lat_q = SEQ // TQ
    ctx_base = batch * SEQ // CTX_LEN
    qrow = _q_row_block(n_lat_q, ctx_base)
    hw = MLA_HEADS * LANE
    vw = MLA_HEADS * MLA_V
    return pl.pallas_call(
        functools.partial(_mla_attn_kernel, n_lat_q=n_lat_q),
        name="mla_attn",
        grid=(batch, n_lat_q + (1 if need_ctx else 0)),
        in_specs=[
            pl.BlockSpec((TQ, hw), lambda b, t: (qrow(b, t), 0)),
            pl.BlockSpec((SEQ, hw), lambda b, t: (b, 0)),
            pl.BlockSpec((CTX_LEN, hw), lambda b, t: (ctx_base + b, 0)),
            pl.BlockSpec((SEQ, vw), lambda b, t: (b, 0)),
            pl.BlockSpec((CTX_LEN, vw), lambda b, t: (ctx_base + b, 0)),
        ],
        out_specs=pl.BlockSpec((TQ, vw), lambda b, t: (qrow(b, t), 0)),
        out_shape=jax.ShapeDtypeStruct((ntok, vw), BF16),
        compiler_params=_cparams(("parallel", "arbitrary")),
    )(q, k, k, v, v)


def _gla_kernel(ql_ref, qc_ref, kl_ref, kc_ref, vl_ref, vc_ref, gl_ref, gc_ref, lrl_ref, lrc_ref,
                wg_ref, bg_ref, nw_ref, yl_ref, yc_ref,
                q_s, k_s, v_s, la_s, of_s, ob_s, st_s):
    n_tok = SEQ + CTX_LEN
    n_chunk = n_tok // GLA_CHUNK
    ctx_chunks = CTX_LEN // GLA_CHUNK
    ck = GLA_CHUNK
    qk_w = GLA_QK_W
    v_w = GLA_V_W

    q_s[0:SEQ, :] = ql_ref[...]
    q_s[SEQ:n_tok, :] = qc_ref[...]
    k_s[0:SEQ, :] = kl_ref[...]
    k_s[SEQ:n_tok, :] = kc_ref[...]
    v_s[0:SEQ, :] = vl_ref[...]
    v_s[SEQ:n_tok, :] = vc_ref[...]

    def log_decay(lr):
        zz = _dot(lr, wg_ref[...]) + bg_ref[...]
        return (jnp.minimum(zz, 0.0) - jnp.log1p(jnp.exp(-jnp.abs(zz)))) * (1.0 / GLA_GATE_TAU)

    rt = 256
    for i in range(SEQ // rt):
        la_s[i * rt:(i + 1) * rt, :] = log_decay(lrl_ref[i * rt:(i + 1) * rt, :])
    la_s[SEQ:n_tok, :] = log_decay(lrc_ref[...])

    st_s[...] = jnp.zeros_like(st_s)

    row = lax.broadcasted_iota(jnp.int32, (ck, 3 * ck), 0)
    col = lax.broadcasted_iota(jnp.int32, (ck, 3 * ck), 1) % ck
    tri3_f = jnp.where(row >= col, 1.0, 0.0).astype(BF16)
    tri3_b = jnp.where(row <= col, 1.0, 0.0).astype(BF16)
    c_i = lax.broadcasted_iota(jnp.int32, (ck, qk_w), 0)
    s_i = lax.broadcasted_iota(jnp.int32, (ck, qk_w), 1) % ck
    keep_f = s_i <= c_i
    keep_b = s_i >= c_i
    bd_k = (lax.broadcasted_iota(jnp.int32, (qk_w, qk_w), 0) // ck
            == lax.broadcasted_iota(jnp.int32, (qk_w, qk_w), 1) // GLA_DK)
    bd_v = (lax.broadcasted_iota(jnp.int32, (qk_w, v_w), 0) // ck
            == lax.broadcasted_iota(jnp.int32, (qk_w, v_w), 1) // GLA_DV)
    bd_s = (lax.broadcasted_iota(jnp.int32, (v_w, qk_w), 0) // GLA_DV
            == lax.broadcasted_iota(jnp.int32, (v_w, qk_w), 1) // GLA_DK)

    def chunk_step(n, direction, tri3, keep, la_lo, o_s):
        r0 = pl.multiple_of(n * ck, ck)
        q = q_s[pl.ds(r0, ck), :].astype(F32) * (GLA_DK ** -0.5)
        k = k_s[pl.ds(r0, ck), :].astype(F32)
        v = v_s[pl.ds(r0, ck), :]
        la = la_s[pl.ds(r0, ck), la_lo:la_lo + qk_w]
        cum = _dot(tri3, jnp.concatenate(_split3(la), axis=0))
        last = cum[ck - 1:ck, :] if direction == 0 else cum[0:1, :]
        q_dec = (q * jnp.exp(cum)).astype(BF16)
        k_inv = (k * jnp.exp(-cum)).astype(BF16)
        k_end = (k * jnp.exp(last - cum)).astype(BF16)
        k_bd = jnp.where(bd_k, jnp.concatenate([k_inv] * GLA_HEADS, axis=0), jnp.zeros((), BF16))
        att = jnp.where(keep, _dot_nt(q_dec, k_bd), 0.0).astype(BF16)
        v_bd = jnp.where(bd_v, jnp.concatenate([v] * GLA_HEADS, axis=0), jnp.zeros((), BF16))
        st = st_s[direction]
        o = _dot(att, v_bd) + _dot_nt(q_dec, st.astype(BF16))
        o_s[pl.ds(r0, ck), :] = o
        kv_t = _dot_tn(v, k_end)
        st_s[direction] = st * jnp.exp(last) + jnp.where(bd_s, kv_t, 0.0)

    def body(i, carry):
        nf = jnp.where(i < ctx_chunks, n_chunk - ctx_chunks + i, i - ctx_chunks)
        nb = n_chunk - 1 - i
        chunk_step(nf, 0, tri3_f, keep_f, 0, of_s)
        chunk_step(nb, 1, tri3_b, keep_b, qk_w, ob_s)
        return carry

    lax.fori_loop(0, n_chunk, body, 0)

    nw = nw_ref[...]

    def finish(r0, rows, g):
        o = of_s[r0:r0 + rows, :] + ob_s[r0:r0 + rows, :]
        g = g.astype(F32)
        parts = []
        for h in range(GLA_HEADS):
            parts.append(_rms(o[:, h * GLA_DV:(h + 1) * GLA_DV], nw))
        y = jnp.concatenate(parts, axis=1)
        return (y * (g * _sigmoid(g))).astype(BF16)

    for i in range(SEQ // rt):
        yl_ref[i * rt:(i + 1) * rt, :] = finish(i * rt, rt, gl_ref[i * rt:(i + 1) * rt, :])
    yc_ref[...] = finish(SEQ, CTX_LEN, gc_ref[...])


def _gla(z, wg, bg, nw, batch):
    ctx_base = batch * SEQ // CTX_LEN
    n_tok = SEQ + CTX_LEN

    def lat(width, col0):
        return pl.BlockSpec((SEQ, width), lambda b: (b, col0 // width))

    def ctx(width, col0):
        return pl.BlockSpec((CTX_LEN, width), lambda b: (ctx_base + b, col0 // width))

    const = lambda b: (0, 0)
    return pl.pallas_call(
        _gla_kernel,
        name="gla_scan",
        grid=(batch,),
        in_specs=[
            lat(GLA_QK_W, Z_BQ), ctx(GLA_QK_W, Z_BQ),
            lat(GLA_QK_W, Z_BK), ctx(GLA_QK_W, Z_BK),
            lat(GLA_V_W, Z_BV), ctx(GLA_V_W, Z_BV),
            lat(GLA_V_W, Z_BG), ctx(GLA_V_W, Z_BG),
            lat(LANE, Z_LR), ctx(LANE, Z_LR),
            pl.BlockSpec((LANE, 2 * GLA_QK_W), const),
            pl.BlockSpec((1, 2 * GLA_QK_W), const),
            pl.BlockSpec((1, GLA_DV), const),
        ],
        out_specs=[
            pl.BlockSpec((SEQ, GLA_V_W), lambda b: (b, 0)),
            pl.BlockSpec((CTX_LEN, GLA_V_W), lambda b: (b, 0)),
        ],
        out_shape=[
            jax.ShapeDtypeStruct((batch * SEQ, GLA_V_W), BF16),
            jax.ShapeDtypeStruct((batch * CTX_LEN, GLA_V_W), BF16),
        ],
        scratch_shapes=[
            pltpu.VMEM((n_tok, GLA_QK_W), BF16),
            pltpu.VMEM((n_tok, GLA_QK_W), BF16),
            pltpu.VMEM((n_tok, GLA_V_W), BF16),
            pltpu.VMEM((n_tok, 2 * GLA_QK_W), F32),
            pltpu.VMEM((n_tok, GLA_V_W), F32),
            pltpu.VMEM((n_tok, GLA_V_W), F32),
            pltpu.VMEM((2, GLA_V_W, GLA_QK_W), F32),
        ],
        compiler_params=_cparams(("parallel",)),
    )(z, z, z, z, z, z, z, z, z, z, wg, bg, nw)


def _na_kernel(q_ref, kl_ref, kc_ref, vl_ref, vc_ref, bias_ref, o_ref, *, n_lat_q):
    lane = lax.broadcasted_iota(jnp.int32, (TQ, LANE), 1)
    t = pl.program_id(1)

    def attend(with_band):
        if with_band:
            r0 = jnp.clip(NA_QROWS * t - NA_KH // 2, 0, GRID_ROWS - NA_BAND_ROWS)
            start = pl.multiple_of(r0 * GRID_W, GRID_W)
        for p in range(NA_HEADS // 2):
            cols = slice(p * LANE, (p + 1) * LANE)
            q2 = q_ref[:, cols] * jnp.asarray(NA_SCALE, BF16)
            kc = kc_ref[:, cols]
            vc = vc_ref[:, cols]
            if with_band:
                kb = kl_ref[pl.ds(start, NA_BAND), cols]
                vb = vl_ref[pl.ds(start, NA_BAND), cols]
            outs = []
            for hh in range(2):
                head_lanes = lane < NA_HEAD_DIM if hh == 0 else lane >= NA_HEAD_DIM
                qm = jnp.where(head_lanes, q2, jnp.zeros((), BF16))
                s_parts = [_dot_nt(qm, kc)]
                v_parts = [vc]
                if with_band:
                    s_parts.append(_dot_nt(qm, kb) + bias_ref[0, 2 * p + hh])
                    v_parts.append(vb)
                outs.append(_softmax_pv(s_parts, v_parts))
            o_ref[:, cols] = jnp.where(lane < NA_HEAD_DIM, outs[0], outs[1]).astype(BF16)

    pl.when(t < n_lat_q)(lambda: attend(True))
    pl.when(t >= n_lat_q)(lambda: attend(False))


def _na(z, bias, batch, need_ctx):
    ntok = z.shape[0]
    n_lat_q = SEQ // TQ
    ctx_base = batch * SEQ // CTX_LEN
    qrow = _q_row_block(n_lat_q, ctx_base)
    w = NA_W
    bias_type = lambda b, t: (jnp.where(t == 0, 0, jnp.where(t >= n_lat_q - 1, 2, 1)), 0, 0, 0)
    return pl.pallas_call(
        functools.partial(_na_kernel, n_lat_q=n_lat_q),
        name="na_attn",
        grid=(batch, n_lat_q + (1 if need_ctx else 0)),
        in_specs=[
            pl.BlockSpec((TQ, w), lambda b, t: (qrow(b, t), Z_CQ // w)),
            pl.BlockSpec((SEQ, w), lambda b, t: (b, Z_CK // w)),
            pl.BlockSpec((CTX_LEN, w), lambda b, t: (ctx_base + b, Z_CK // w)),
            pl.BlockSpec((SEQ, w), lambda b, t: (b, Z_CV // w)),
            pl.BlockSpec((CTX_LEN, w), lambda b, t: (ctx_base + b, Z_CV // w)),
            pl.BlockSpec((1, NA_HEADS, NA_QBLK, NA_BAND), bias_type),
        ],
        out_specs=pl.BlockSpec((TQ, w), lambda b, t: (qrow(b, t), 0)),
        out_shape=jax.ShapeDtypeStruct((ntok, w), BF16),
        compiler_params=_cparams(("parallel", "arbitrary")),
    )(z, z, z, z, z, bias)


def _merge_kernel(h_ref, ya_ref, ybl_ref, ybc_ref, yc_ref, ga_ref, gb_ref, gc_ref, mod_ref,
                  wa_ref, wb_ref, wc_ref, wo_ref, o_ref, *, n_lat_tiles):
    is_lat = pl.program_id(0) < n_lat_tiles
    yb = jnp.where(is_lat, ybl_ref[...], ybc_ref[...])
    m = (_sigmoid(ga_ref[...].astype(F32)) * _dot(ya_ref[...], wa_ref[...])
         + _sigmoid(gb_ref[...].astype(F32)) * _dot(yb, wb_ref[...])
         + _sigmoid(gc_ref[...].astype(F32)) * _dot(yc_ref[...], wc_ref[...]))
    o_ref[...] = h_ref[...] + mod_ref[0, 2:3, :] * _dot(m.astype(BF16), wo_ref[...])


def _merge(h, ya, yb_lat, yb_ctx, yc, z, mod, wa, wb, wc, wo, batch, need_ctx):
    ntok, d = h.shape
    tm = TM_TOK
    n_lat_tiles = batch * SEQ // tm
    n_tiles = ntok // tm if need_ctx else n_lat_tiles
    per_batch = SEQ // tm
    bw = wa.shape[0]
    const = lambda i: (0, 0)
    tok = lambda width: pl.BlockSpec((tm, width), lambda i: (i, 0))
    gate = lambda g: pl.BlockSpec((tm, d), lambda i: (i, Z_GATES // d + g))
    return pl.pallas_call(
        functools.partial(_merge_kernel, n_lat_tiles=n_lat_tiles),
        name="merge",
        grid=(n_tiles,),
        in_specs=[
            tok(d), tok(bw),
            pl.BlockSpec((tm, bw), lambda i: (jnp.minimum(i, n_lat_tiles - 1), 0)),
            pl.BlockSpec((tm, bw), lambda i: (jnp.maximum(i - n_lat_tiles, 0), 0)),
            tok(bw), gate(0), gate(1), gate(2),
            pl.BlockSpec((1, 6, d), lambda i: (jnp.where(i < n_lat_tiles, i // per_batch, batch), 0, 0)),
            pl.BlockSpec((bw, d), const), pl.BlockSpec((bw, d), const), pl.BlockSpec((bw, d), const),
            pl.BlockSpec((d, d), const),
        ],
        out_specs=tok(d),
        out_shape=jax.ShapeDtypeStruct((ntok, d), F32),
        input_output_aliases={0: 0},
        compiler_params=_cparams(("parallel",)),
    )(h, ya, yb_lat, yb_ctx, yc, z, z, z, mod, wa, wb, wc, wo)


def _ffn_kernel(h_ref, mod_ref, nw_ref, win_ref, wout_ref, fw_ref, o_ref, *, final):
    h = h_ref[...]
    u = (_rms(h, nw_ref[...]) * (1.0 + mod_ref[0, 4:5, :]) + mod_ref[0, 3:4, :]).astype(BF16)
    half = FFN_HIDDEN // 2
    acc = None
    for c in range(2):
        g = _dot(u, win_ref[:, c * half:(c + 1) * half])
        up = _dot(u, win_ref[:, FFN_HIDDEN + c * half:FFN_HIDDEN + (c + 1) * half])
        act = (g * _sigmoid(g) * up).astype(BF16)
        part = _dot(act, wout_ref[c * half:(c + 1) * half, :])
        acc = part if acc is None else acc + part
    out = h + mod_ref[0, 5:6, :] * acc
    if final:
        out = _rms(out, fw_ref[...])
    o_ref[...] = out


def _ffn(h, mod, norm_w, w_in, w_out, final_w, batch, need_ctx, final):
    ntok, d = h.shape
    tm = TM_TOK
    n_lat_tiles = batch * SEQ // tm
    n_tiles = ntok // tm if need_ctx else n_lat_tiles
    per_batch = SEQ // tm
    const = lambda i: (0, 0)
    resident = dict(pipeline_mode=pl.Buffered(1))
    out_rows = n_tiles * tm if final else ntok
    return pl.pallas_call(
        functools.partial(_ffn_kernel, final=final),
        name="ffn",
        grid=(n_tiles,),
        in_specs=[
            pl.BlockSpec((tm, d), lambda i: (i, 0)),
            pl.BlockSpec((1, 6, d), lambda i: (jnp.where(i < n_lat_tiles, i // per_batch, batch), 0, 0)),
            pl.BlockSpec((1, d), const),
            pl.BlockSpec((d, 2 * FFN_HIDDEN), const, **resident),
            pl.BlockSpec((FFN_HIDDEN, d), const, **resident),
            pl.BlockSpec((1, d), const),
        ],
        out_specs=pl.BlockSpec((tm, d), lambda i: (i, 0)),
        out_shape=jax.ShapeDtypeStruct((out_rows, d), F32),
        input_output_aliases={} if final else {0: 0},
        compiler_params=_cparams(("parallel",)),
    )(h, mod, norm_w, w_in, w_out, final_w)


def _pad_cols(w, width):
    return jnp.pad(w, ((0, 0), (0, 0), (0, width - w.shape[-1])))


def _prep_w_in(w_in):
    a0 = 0
    b0 = MLA_Q_RANK + MLA_KV_RANK + MLA_ROPE
    c0 = b0 + 2 * GLA_QK_W + 2 * GLA_V_W + 2 * GLA_GATE_RANK
    g0 = c0 + 3 * NA_W
    sl = lambda lo, n: w_in[:, :, lo:lo + n]
    aq = sl(a0, MLA_Q_RANK)
    akv = sl(a0 + MLA_Q_RANK, MLA_KV_RANK)
    kr0 = a0 + MLA_Q_RANK + MLA_KV_RANK
    half = MLA_ROPE // 2
    kr = sl(kr0, MLA_ROPE)
    kr_sw = jnp.concatenate([sl(kr0 + half, half), sl(kr0, half)], axis=-1)
    zeros = lambda n: jnp.zeros(w_in.shape[:2] + (n,), w_in.dtype)
    akr = jnp.concatenate([zeros(MLA_NOPE), kr, zeros(LANE - MLA_NOPE - MLA_ROPE),
                           zeros(MLA_NOPE), kr_sw, zeros(LANE - MLA_NOPE - MLA_ROPE)], axis=-1)
    bq = sl(b0, GLA_QK_W)
    bk = sl(b0 + GLA_QK_W, GLA_QK_W)
    bv = sl(b0 + 2 * GLA_QK_W, GLA_V_W)
    bg = sl(b0 + 2 * GLA_QK_W + GLA_V_W, GLA_V_W)
    lr = _pad_cols(sl(b0 + 2 * GLA_QK_W + 2 * GLA_V_W, 2 * GLA_GATE_RANK), LANE)
    cq = sl(c0, NA_W)
    ck = sl(c0 + NA_W, NA_W)
    cv = sl(c0 + 2 * NA_W, NA_W)
    gates = sl(g0, 3 * D_MODEL)
    out = jnp.concatenate([aq, lr, cq, ck, cv, bv, bg, bq, bk, akv, akr, gates], axis=-1)
    assert out.shape[-1] == Z_WIDTH
    return out.astype(BF16)


def _prep_mla_weights(w_q_up, w_kv_up):
    depth = w_q_up.shape[0]
    hd = MLA_NOPE + MLA_ROPE
    half = MLA_ROPE // 2
    wq = w_q_up.reshape(depth, MLA_Q_RANK, MLA_HEADS, hd)
    pad = jnp.zeros((depth, MLA_Q_RANK, MLA_HEADS, LANE - hd), w_q_up.dtype)
    plain = jnp.concatenate([wq, pad], axis=-1)
    swapped = jnp.concatenate([jnp.zeros_like(wq[..., :MLA_NOPE]), wq[..., MLA_NOPE + half:],
                               wq[..., MLA_NOPE:MLA_NOPE + half], pad], axis=-1)
    wq_p = jnp.concatenate([plain.reshape(depth, MLA_Q_RANK, -1), swapped.reshape(depth, MLA_Q_RANK, -1)], axis=-1)
    wkv = w_kv_up.reshape(depth, MLA_KV_RANK, MLA_HEADS, MLA_NOPE + MLA_V)
    wkn = jnp.concatenate([wkv[..., :MLA_NOPE],
                           jnp.zeros((depth, MLA_KV_RANK, MLA_HEADS, LANE - MLA_NOPE), w_kv_up.dtype)], axis=-1)
    wv = wkv[..., MLA_NOPE:]
    return (wq_p.astype(BF16), wkn.reshape(depth, MLA_KV_RANK, -1).astype(BF16),
            wv.reshape(depth, MLA_KV_RANK, -1).astype(BF16))


def _rope_tables():
    t = np.arange(SEQ)
    rows = (t // GRID_W).astype(np.float32)
    cols = (t % GRID_W).astype(np.float32)
    n_freq = MLA_ROPE // 4
    inv_freq = jnp.asarray(ROPE_BASE, F32) ** (-jnp.arange(n_freq, dtype=F32) / n_freq)
    ang = jnp.concatenate([jnp.asarray(rows)[:, None] * inv_freq, jnp.asarray(cols)[:, None] * inv_freq], axis=-1)
    cos, sin = jnp.cos(ang), jnp.sin(ang)
    cos = jnp.concatenate([cos, jnp.ones((TM_IN, MLA_ROPE // 2), F32)], axis=0)
    sin = jnp.concatenate([sin, jnp.zeros((TM_IN, MLA_ROPE // 2), F32)], axis=0)
    n = cos.shape[0]
    tail = jnp.zeros((n, LANE - MLA_NOPE - MLA_ROPE), F32)
    c_rot = jnp.concatenate([cos, cos], axis=-1)
    s_rot = jnp.concatenate([-sin, sin], axis=-1)
    nope1 = jnp.ones((n, MLA_NOPE), F32)
    nope0 = jnp.zeros((n, MLA_NOPE), F32)
    tqc = jnp.concatenate([nope1, c_rot, tail], axis=-1) * MLA_SCALE
    tqs = jnp.concatenate([nope0, s_rot, tail], axis=-1) * MLA_SCALE
    tkc = jnp.concatenate([nope0, c_rot, tail], axis=-1)
    tks = jnp.concatenate([nope0, s_rot, tail], axis=-1)
    return tqc, tqs, tkc, tks


def _na_bias_tables(rpb):
    n_blocks = GRID_ROWS // NA_QROWS
    tables = []
    for g in (0, 1, n_blocks - 1):
        band0 = int(np.clip(NA_QROWS * g - NA_KH // 2, 0, GRID_ROWS - NA_BAND_ROWS))
        qr = NA_QROWS * g + np.arange(NA_QROWS)
        kr = band0 + np.arange(NA_BAND_ROWS)
        r0 = np.clip(qr - NA_KH // 2, 0, GRID_ROWS - NA_KH)
        row_ok = (kr[None, :] >= r0[:, None]) & (kr[None, :] < r0[:, None] + NA_KH)
        row_off = np.clip(kr[None, :] - qr[:, None] + NA_KH - 1, 0, 2 * NA_KH - 2)
        col = np.arange(GRID_W)
        cs = np.clip(col - NA_KW // 2, 0, GRID_W - NA_KW)
        col_ok = (col[None, :] >= cs[:, None]) & (col[None, :] < cs[:, None] + NA_KW)
        col_off = np.clip(col[None, :] - col[:, None] + NA_KW - 1, 0, 2 * NA_KW - 2)
        ok = row_ok[:, None, :, None] & col_ok[None, :, None, :]
        vals = rpb[:, :, row_off[:, None, :, None], col_off[None, :, None, :]]
        tab = jnp.where(jnp.asarray(ok), vals.astype(F32), -1e30)
        tables.append(tab.reshape(rpb.shape[0], NA_HEADS, NA_QBLK, NA_BAND))
    return jnp.stack(tables, axis=1)


def _prep_gla_gate(w_f, b_f, w_b, b_b):
    depth = w_f.shape[0]
    wg = jnp.zeros((depth, LANE, 2 * GLA_QK_W), F32)
    wg = wg.at[:, :GLA_GATE_RANK, :GLA_QK_W].set(w_f)
    wg = wg.at[:, GLA_GATE_RANK:2 * GLA_GATE_RANK, GLA_QK_W:].set(w_b)
    bg = jnp.concatenate([b_f, b_b], axis=-1)[:, None, :]
    return wg.astype(BF16), bg


def kernel(x, c, ctx, c_ctx, w_mod, b_mod, norm1_w, w_in, mla_q_norm_w, mla_kv_norm_w, mla_w_q_up, mla_w_kv_up, gla_w_gate_f, gla_b_gate_f, gla_w_gate_b, gla_b_gate_b, gla_norm_w, na_rpb, w_a_o, w_b_o, w_c_o, w_out, norm2_w, w_ffn_in, w_ffn_out, final_norm_w):
    batch, seq, d = x.shape
    depth = w_mod.shape[0]
    assert seq == SEQ and d == D_MODEL and ctx.shape[1] == CTX_LEN
    assert (batch * CTX_LEN) % TM_IN == 0
    n_lat_in = batch * SEQ // TM_IN

    mod_rows = -(-(batch + 1) // 8) * 8
    c_all = jnp.concatenate([c, c_ctx[None, :], jnp.zeros((mod_rows - batch - 1, d), F32)], axis=0)
    mod_all = _modulation(c_all, w_mod, b_mod).reshape(depth, mod_rows, 6, d)

    w_in_p = _prep_w_in(w_in)
    wq_p, wkn_p, wv_p = _prep_mla_weights(mla_w_q_up, mla_w_kv_up)
    tabs = _rope_tables()
    na_bias = _na_bias_tables(na_rpb)
    wg_p, bg_p = _prep_gla_gate(gla_w_gate_f, gla_b_gate_f, gla_w_gate_b, gla_b_gate_b)
    wa, wb, wc, wo = (w.astype(BF16) for w in (w_a_o, w_b_o, w_c_o, w_out))
    wfi, wfo = w_ffn_in.astype(BF16), w_ffn_out.astype(BF16)
    row = lambda w, i: w[i][None, :]

    h = jnp.concatenate([x.reshape(batch * SEQ, d), ctx.reshape(batch * CTX_LEN, d)], axis=0)
    for i in range(depth):
        need_ctx = i < depth - 1
        mod = mod_all[i]
        z = _inproj(h, mod, row(norm1_w, i), w_in_p[i], n_lat_in, batch)
        q, k, v = _mla_prep(z, row(mla_q_norm_w, i), row(mla_kv_norm_w, i), wq_p[i], wkn_p[i], wv_p[i], tabs, n_lat_in)
        ya = _mla_attn(q, k, v, batch, need_ctx)
        yb_lat, yb_ctx = _gla(z, wg_p[i], bg_p[i], row(gla_norm_w, i), batch)
        yc = _na(z, na_bias[i], batch, need_ctx)
        h = _merge(h, ya, yb_lat, yb_ctx, yc, z, mod, wa[i], wb[i], wc[i], wo[i], batch, need_ctx)
        h = _ffn(h, mod, row(norm2_w, i), wfi[i], wfo[i], final_norm_w[None, :], batch, need_ctx, not need_ctx)
    return h.reshape(batch, SEQ, d)
```

```python
import functools

import jax
import jax.numpy as jnp
import numpy as np
from jax import lax
from jax.experimental import pallas as pl
from jax.experimental.pallas import tpu as pltpu

F32 = jnp.float32
BF16 = jnp.bfloat16

D_MODEL = 1024
SEQ = 2048
CTX_LEN = 256
GRID_W = 64
GRID_ROWS = SEQ // GRID_W
EPS = 1e-6
ROPE_BASE = 10000.0

MLA_HEADS = 8
MLA_Q_RANK = 384
MLA_KV_RANK = 256
MLA_NOPE = 64
MLA_ROPE = 32
MLA_V = 64
MLA_SCALE = (MLA_NOPE + MLA_ROPE) ** -0.5

GLA_HEADS = 4
GLA_DK = 64
GLA_DV = 128
GLA_GATE_RANK = 16
GLA_GATE_TAU = 16.0
GLA_CHUNK = 64
GLA_QK_W = GLA_HEADS * GLA_DK
GLA_V_W = GLA_HEADS * GLA_DV

NA_HEADS = 8
NA_HEAD_DIM = 64
NA_KH = 8
NA_KW = 16
NA_W = NA_HEADS * NA_HEAD_DIM
NA_SCALE = NA_HEAD_DIM ** -0.5
LOG2E = 1.4426950408889634
NA_QROWS = 4
NA_BAND_ROWS = NA_KH + NA_QROWS
NA_BAND = NA_BAND_ROWS * GRID_W
NA_QBLK = NA_QROWS * GRID_W

FFN_HIDDEN = 2816
LANE = 128
BF16_SUBLANES = 16

Z_AQ = 0
Z_LR = 384
Z_CQ = 512
Z_CK = 1024
Z_CV = 1536
Z_BV = 2048
Z_BG = 2560
Z_BQ = 3072
Z_BK = 3328
Z_AKV = 3584
Z_AKR = 3840
Z_GATES = 4096
Z_WIDTH = 7168

TM_IN = 1024
TN_IN = 1024
TM_TOK = 512
TQ = 256
KEY_CHUNK = 256
MLA_KEY_CHUNK = 512
HEAD_GROUP = 8
VMEM_LIMIT = 56 * 1024 * 1024


def _cparams(sem):
    return pltpu.CompilerParams(dimension_semantics=sem, vmem_limit_bytes=VMEM_LIMIT)


def _dot(a, b):
    return jnp.dot(a, b, preferred_element_type=F32)


def _dot_nt(a, b):
    return lax.dot_general(a, b, (((1,), (1,)), ((), ())), preferred_element_type=F32)


def _dot_tn(a, b):
    return lax.dot_general(a, b, (((0,), (0,)), ((), ())), preferred_element_type=F32)


def _sigmoid(x):
    return 1.0 / (1.0 + jnp.exp(-x))


def _rms(x, w):
    return x * lax.rsqrt(jnp.mean(x * x, axis=-1, keepdims=True) + EPS) * w


def _split3(x):
    hi = x.astype(BF16)
    r1 = x - hi.astype(F32)
    mid = r1.astype(BF16)
    lo = (r1 - mid.astype(F32)).astype(BF16)
    return hi, mid, lo


def _mod_kernel(c_ref, w_ref, b_ref, o_ref):
    c = c_ref[...]
    a = c * _sigmoid(c)
    a_hi = a.astype(BF16)
    a_lo = (a - a_hi.astype(F32)).astype(BF16)
    w = w_ref[0]
    w_hi = w.astype(BF16)
    w_lo = (w - w_hi.astype(F32)).astype(BF16)
    o_ref[0] = _dot(a_hi, w_hi) + _dot(a_lo, w_hi) + _dot(a_hi, w_lo) + b_ref[0]


def _modulation(c_all, w_mod, b_mod):
    depth, d, n = w_mod.shape
    rows = c_all.shape[0]
    tn = 1536
    return pl.pallas_call(
        _mod_kernel,
        name="adaln_mod",
        grid=(depth, n // tn),
        in_specs=[
            pl.BlockSpec((rows, d), lambda l, j: (0, 0)),
            pl.BlockSpec((1, d, tn), lambda l, j: (l, 0, j)),
            pl.BlockSpec((1, 1, tn), lambda l, j: (l, 0, j)),
        ],
        out_specs=pl.BlockSpec((1, rows, tn), lambda l, j: (l, 0, j)),
        out_shape=jax.ShapeDtypeStruct((depth, rows, n), F32),
        compiler_params=_cparams(("parallel", "parallel")),
    )(c_all, w_mod, b_mod.reshape(depth, 1, n))


def _inproj_kernel(h_ref, mod_ref, nw_ref, w_ref, o_ref, xn_ref):
    @pl.when(pl.program_id(1) == 0)
    def _():
        y = _rms(h_ref[...], nw_ref[...])
        xn_ref[...] = (y * (1.0 + mod_ref[0, 1:2, :]) + mod_ref[0, 0:1, :]).astype(BF16)

    o_ref[...] = _dot(xn_ref[...], w_ref[...]).astype(BF16)


def _inproj(h, mod, norm_w, w_in_p, n_lat_tiles, ctx_row):
    ntok, d = h.shape
    per_batch = SEQ // TM_IN
    return pl.pallas_call(
        _inproj_kernel,
        name="in_proj",
        grid=(ntok // TM_IN, Z_WIDTH // TN_IN),
        in_specs=[
            pl.BlockSpec((TM_IN, d), lambda i, j: (i, 0)),
            pl.BlockSpec((1, 6, d), lambda i, j: (jnp.where(i < n_lat_tiles, i // per_batch, ctx_row), 0, 0)),
            pl.BlockSpec((1, d), lambda i, j: (0, 0)),
            pl.BlockSpec((d, TN_IN), lambda i, j: (0, j)),
        ],
        out_specs=pl.BlockSpec((TM_IN, TN_IN), lambda i, j: (i, j)),
        out_shape=jax.ShapeDtypeStruct((ntok, Z_WIDTH), BF16),
        scratch_shapes=[pltpu.VMEM((TM_IN, d), BF16)],
        compiler_params=_cparams(("parallel", "arbitrary")),
    )(h, mod, norm_w, w_in_p)


def _mla_prep_kernel(aq_ref, akv_ref, akr_ref, qnw_ref, kvnw_ref, wq_ref, wkn_ref, wv_ref,
                     tqc_ref, tqs_ref, tkc_ref, tks_ref, q_out, k_out, v_out):
    qn = _rms(aq_ref[...].astype(F32), qnw_ref[...]).astype(BF16)
    qq = _dot(qn, wq_ref[...])
    hw = MLA_HEADS * LANE
    tqc = tqc_ref[...]
    tqs = tqs_ref[...]
    for h in range(MLA_HEADS):
        lo = h * LANE
        q_out[:, lo:lo + LANE] = (qq[:, lo:lo + LANE] * tqc + qq[:, hw + lo:hw + lo + LANE] * tqs).astype(BF16)
    kvn = _rms(akv_ref[...].astype(F32), kvnw_ref[...]).astype(BF16)
    kn = _dot(kvn, wkn_ref[...])
    r = akr_ref[...].astype(F32)
    kr = r[:, :LANE] * tkc_ref[...] + r[:, LANE:] * tks_ref[...]
    for h in range(MLA_HEADS):
        lo = h * LANE
        k_out[:, lo:lo + LANE] = (kn[:, lo:lo + LANE] + kr).astype(BF16)
    v_out[...] = _dot(kvn, wv_ref[...]).astype(BF16)


def _mla_prep(z, qnw, kvnw, wq, wkn, wv, tabs, n_lat_tiles):
    ntok = z.shape[0]
    tm = TM_IN
    per_batch = SEQ // tm
    tab_idx = lambda i: (jnp.where(i < n_lat_tiles, i % per_batch, per_batch), 0)
    const = lambda i: (0, 0)
    hw = MLA_HEADS * LANE
    tab_spec = pl.BlockSpec((tm, LANE), tab_idx)
    return pl.pallas_call(
        _mla_prep_kernel,
        name="mla_prep",
        grid=(ntok // tm,),
        in_specs=[
            pl.BlockSpec((tm, MLA_Q_RANK), lambda i: (i, Z_AQ // MLA_Q_RANK)),
            pl.BlockSpec((tm, MLA_KV_RANK), lambda i: (i, Z_AKV // MLA_KV_RANK)),
            pl.BlockSpec((tm, 2 * LANE), lambda i: (i, Z_AKR // (2 * LANE))),
            pl.BlockSpec((1, MLA_Q_RANK), const),
            pl.BlockSpec((1, MLA_KV_RANK), const),
            pl.BlockSpec((MLA_Q_RANK, 2 * hw), const),
            pl.BlockSpec((MLA_KV_RANK, hw), const),
            pl.BlockSpec((MLA_KV_RANK, MLA_HEADS * MLA_V), const),
            tab_spec, tab_spec, tab_spec, tab_spec,
        ],
        out_specs=[
            pl.BlockSpec((tm, hw), lambda i: (i, 0)),
            pl.BlockSpec((tm, hw), lambda i: (i, 0)),
            pl.BlockSpec((tm, MLA_HEADS * MLA_V), lambda i: (i, 0)),
        ],
        out_shape=[
            jax.ShapeDtypeStruct((ntok, hw), BF16),
            jax.ShapeDtypeStruct((ntok, hw), BF16),
            jax.ShapeDtypeStruct((ntok, MLA_HEADS * MLA_V), BF16),
        ],
        compiler_params=_cparams(("parallel",)),
    )(z, z, z, qnw, kvnw, wq, wkn, wv, *tabs)


def _attend_t(qs, n_chunks, chunk):
    if len(qs) > HEAD_GROUP:
        outs = []
        for g in range(0, len(qs), HEAD_GROUP):
            outs += _attend_t(qs[g:g + HEAD_GROUP], n_chunks, lambda h, c, g=g: chunk(g + h, c))
        return outs
    heads = range(len(qs))
    dv = chunk(0, 0)[1].shape[0]
    m = o = None
    for c in range(n_chunks):
        parts = [chunk(h, c) for h in heads]
        s = [_dot_nt(parts[h][0], qs[h]) for h in heads]
        s = [s[h] if parts[h][2] is None else s[h] + parts[h][2] for h in heads]
        mc = [jnp.max(s[h], axis=0, keepdims=True) for h in heads]
        ones = jnp.ones((BF16_SUBLANES, parts[0][1].shape[1]), BF16)
        vt1 = [jnp.concatenate([parts[h][1], ones], axis=0) for h in heads]
        if m is None:
            m = mc
            o = [_dot(vt1[h], jnp.exp2(s[h] - m[h]).astype(BF16)) for h in heads]
        else:
            m_new = [jnp.maximum(m[h], mc[h]) for h in heads]
            alpha = [jnp.exp2(m[h] - m_new[h]) for h in heads]
            o = [o[h] * alpha[h] + _dot(vt1[h], jnp.exp2(s[h] - m_new[h]).astype(BF16)) for h in heads]
            m = m_new
    return [o[h][:dv] / o[h][dv:dv + 1] for h in heads]


def _transpose_rows(src_ref, dst_ref, rows):
    for r in range(0, rows, TQ):
        dst_ref[:, r:r + TQ] = src_ref[r:r + TQ, :].astype(F32).T.astype(BF16)


def _mla_attn_kernel(q_ref, kl_ref, kc_ref, vl_ref, vc_ref, o_ref, vlt_s, vct_s, *, n_lat_q):
    t = pl.program_id(1)

    @pl.when(t == 0)
    def _():
        _transpose_rows(vl_ref, vlt_s, SEQ)
        _transpose_rows(vc_ref, vct_s, CTX_LEN)

    def attend(with_lat):
        def chunk(h, c):
            cols = slice(h * LANE, (h + 1) * LANE)
            rows = slice(h * MLA_V, (h + 1) * MLA_V)
            if c == 0:
                return kc_ref[:, cols], vct_s[rows, :], None
            keys = slice((c - 1) * MLA_KEY_CHUNK, c * MLA_KEY_CHUNK)
            return kl_ref[keys, cols], vlt_s[rows, keys], None

        qs = [q_ref[:, h * LANE:(h + 1) * LANE] for h in range(MLA_HEADS)]
        outs = _attend_t(qs, 1 + (SEQ // MLA_KEY_CHUNK if with_lat else 0), chunk)
        o_ref[...] = jnp.concatenate(outs, axis=0).T.astype(BF16)

    pl.when(t < n_lat_q)(lambda: attend(True))
    pl.when(t >= n_lat_q)(lambda: attend(False))


def _q_row_block(n_lat_q, ctx_base):
    return lambda b, t: jnp.where(t < n_lat_q, b * n_lat_q + t, ctx_base + b)


def _mla_attn(q, k, v, batch, need_ctx):
    ntok = q.shape[0]
    n_lat_q = SEQ // TQ
    ctx_base = batch * SEQ // CTX_LEN
    qrow = _q_row_block(n_lat_q, ctx_base)
    hw = MLA_HEADS * LANE
    vw = MLA_HEADS * MLA_V
    return pl.pallas_call(
        functools.partial(_mla_attn_kernel, n_lat_q=n_lat_q),
        name="mla_attn",
        grid=(batch, n_lat_q + (1 if need_ctx else 0)),
        in_specs=[
            pl.BlockSpec((TQ, hw), lambda b, t: (qrow(b, t), 0)),
            pl.BlockSpec((SEQ, hw), lambda b, t: (b, 0)),
            pl.BlockSpec((CTX_LEN, hw), lambda b, t: (ctx_base + b, 0)),
            pl.BlockSpec((SEQ, vw), lambda b, t: (b, 0)),
            pl.BlockSpec((CTX_LEN, vw), lambda b, t: (ctx_base + b, 0)),
        ],
        out_specs=pl.BlockSpec((TQ, vw), lambda b, t: (qrow(b, t), 0)),
        out_shape=jax.ShapeDtypeStruct((ntok, vw), BF16),
        scratch_shapes=[pltpu.VMEM((vw, SEQ), BF16), pltpu.VMEM((vw, CTX_LEN), BF16)],
        compiler_params=_cparams(("parallel", "arbitrary")),
    )(q, k, k, v, v)


def _gla_kernel(ql_ref, qc_ref, kl_ref, kc_ref, vl_ref, vc_ref, gl_ref, gc_ref, lrl_ref, lrc_ref,
                wg_ref, bg_ref, nw_ref, yl_ref, yc_ref,
                q_s, k_s, v_s, la_s, of_s, ob_s, st_s):
    n_tok = SEQ + CTX_LEN
    n_chunk = n_tok // GLA_CHUNK
    ctx_chunks = CTX_LEN // GLA_CHUNK
    ck = GLA_CHUNK
    qk_w = GLA_QK_W
    v_w = GLA_V_W

    q_s[0:SEQ, :] = ql_ref[...]
    q_s[SEQ:n_tok, :] = qc_ref[...]
    k_s[0:SEQ, :] = kl_ref[...]
    k_s[SEQ:n_tok, :] = kc_ref[...]
    v_s[0:SEQ, :] = vl_ref[...]
    v_s[SEQ:n_tok, :] = vc_ref[...]

    def log_decay(lr):
        zz = _dot(lr, wg_ref[...]) + bg_ref[...]
        return (jnp.minimum(zz, 0.0) - jnp.log1p(jnp.exp(-jnp.abs(zz)))) * (1.0 / GLA_GATE_TAU)

    rt = 256
    for i in range(SEQ // rt):
        la_s[i * rt:(i + 1) * rt, :] = log_decay(lrl_ref[i * rt:(i + 1) * rt, :])
    la_s[SEQ:n_tok, :] = log_decay(lrc_ref[...])

    st_s[...] = jnp.zeros_like(st_s)

    row = lax.broadcasted_iota(jnp.int32, (ck, 3 * ck), 0)
    col = lax.broadcasted_iota(jnp.int32, (ck, 3 * ck), 1) % ck
    tri3_f = jnp.where(row >= col, 1.0, 0.0).astype(BF16)
    tri3_b = jnp.where(row <= col, 1.0, 0.0).astype(BF16)
    c_i = lax.broadcasted_iota(jnp.int32, (ck, qk_w), 0)
    s_i = lax.broadcasted_iota(jnp.int32, (ck, qk_w), 1) % ck
    keep_f = s_i <= c_i
    keep_b = s_i >= c_i
    bd_k = (lax.broadcasted_iota(jnp.int32, (qk_w, qk_w), 0) // ck
            == lax.broadcasted_iota(jnp.int32, (qk_w, qk_w), 1) // GLA_DK)
    bd_v = (lax.broadcasted_iota(jnp.int32, (qk_w, v_w), 0) // ck
            == lax.broadcasted_iota(jnp.int32, (qk_w, v_w), 1) // GLA_DV)
    bd_s = (lax.broadcasted_iota(jnp.int32, (v_w, qk_w), 0) // GLA_DV
            == lax.broadcasted_iota(jnp.int32, (v_w, qk_w), 1) // GLA_DK)

    def chunk_step(n, direction, tri3, keep, la_lo, o_s):
        r0 = pl.multiple_of(n * ck, ck)
        q = q_s[pl.ds(r0, ck), :].astype(F32) * (GLA_DK ** -0.5)
        k = k_s[pl.ds(r0, ck), :].astype(F32)
        v = v_s[pl.ds(r0, ck), :]
        la = la_s[pl.ds(r0, ck), la_lo:la_lo + qk_w]
        cum = _dot(tri3, jnp.concatenate(_split3(la), axis=0))
        last = cum[ck - 1:ck, :] if direction == 0 else cum[0:1, :]
        q_dec = (q * jnp.exp(cum)).astype(BF16)
        k_inv = (k * jnp.exp(-cum)).astype(BF16)
        k_end = (k * jnp.exp(last - cum)).astype(BF16)
        k_bd = jnp.where(bd_k, jnp.concatenate([k_inv] * GLA_HEADS, axis=0), jnp.zeros((), BF16))
        att = jnp.where(keep, _dot_nt(q_dec, k_bd), 0.0).astype(BF16)
        v_bd = jnp.where(bd_v, jnp.concatenate([v] * GLA_HEADS, axis=0), jnp.zeros((), BF16))
        st = st_s[direction]
        o = _dot(att, v_bd) + _dot_nt(q_dec, st.astype(BF16))
        o_s[pl.ds(r0, ck), :] = o
        kv_t = _dot_tn(v, k_end)
        st_s[direction] = st * jnp.exp(last) + jnp.where(bd_s, kv_t, 0.0)

    def body(i, carry):
        nf = jnp.where(i < ctx_chunks, n_chunk - ctx_chunks + i, i - ctx_chunks)
        nb = n_chunk - 1 - i
        chunk_step(nf, 0, tri3_f, keep_f, 0, of_s)
        chunk_step(nb, 1, tri3_b, keep_b, qk_w, ob_s)
        return carry

    lax.fori_loop(0, n_chunk, body, 0, unroll=2)

    nw = nw_ref[...]

    def finish(r0, rows, g):
        o = of_s[r0:r0 + rows, :] + ob_s[r0:r0 + rows, :]
        g = g.astype(F32)
        parts = []
        for h in range(GLA_HEADS):
            parts.append(_rms(o[:, h * GLA_DV:(h + 1) * GLA_DV], nw))
        y = jnp.concatenate(parts, axis=1)
        return (y * (g * _sigmoid(g))).astype(BF16)

    for i in range(SEQ // rt):
        yl_ref[i * rt:(i + 1) * rt, :] = finish(i * rt, rt, gl_ref[i * rt:(i + 1) * rt, :])
    yc_ref[...] = finish(SEQ, CTX_LEN, gc_ref[...])


def _gla(z, wg, bg, nw, batch):
    ctx_base = batch * SEQ // CTX_LEN
    n_tok = SEQ + CTX_LEN

    def lat(width, col0):
        return pl.BlockSpec((SEQ, width), lambda b: (b, col0 // width))

    def ctx(width, col0):
        return pl.BlockSpec((CTX_LEN, width), lambda b: (ctx_base + b, col0 // width))

    const = lambda b: (0, 0)
    return pl.pallas_call(
        _gla_kernel,
        name="gla_scan",
        grid=(batch,),
        in_specs=[
            lat(GLA_QK_W, Z_BQ), ctx(GLA_QK_W, Z_BQ),
            lat(GLA_QK_W, Z_BK), ctx(GLA_QK_W, Z_BK),
            lat(GLA_V_W, Z_BV), ctx(GLA_V_W, Z_BV),
            lat(GLA_V_W, Z_BG), ctx(GLA_V_W, Z_BG),
            lat(LANE, Z_LR), ctx(LANE, Z_LR),
            pl.BlockSpec((LANE, 2 * GLA_QK_W), const),
            pl.BlockSpec((1, 2 * GLA_QK_W), const),
            pl.BlockSpec((1, GLA_DV), const),
        ],
        out_specs=[
            pl.BlockSpec((SEQ, GLA_V_W), lambda b: (b, 0)),
            pl.BlockSpec((CTX_LEN, GLA_V_W), lambda b: (b, 0)),
        ],
        out_shape=[
            jax.ShapeDtypeStruct((batch * SEQ, GLA_V_W), BF16),
            jax.ShapeDtypeStruct((batch * CTX_LEN, GLA_V_W), BF16),
        ],
        scratch_shapes=[
            pltpu.VMEM((n_tok, GLA_QK_W), BF16),
            pltpu.VMEM((n_tok, GLA_QK_W), BF16),
            pltpu.VMEM((n_tok, GLA_V_W), BF16),
            pltpu.VMEM((n_tok, 2 * GLA_QK_W), F32),
            pltpu.VMEM((n_tok, GLA_V_W), F32),
            pltpu.VMEM((n_tok, GLA_V_W), F32),
            pltpu.VMEM((2, GLA_V_W, GLA_QK_W), F32),
        ],
        compiler_params=_cparams(("parallel",)),
    )(z, z, z, z, z, z, z, z, z, z, wg, bg, nw)


def _na_kernel(q_ref, kl_ref, kc_ref, vl_ref, vc_ref, bias_ref, o_ref, vlt_s, vct_s, *, n_lat_q):
    lane = lax.broadcasted_iota(jnp.int32, (TQ, LANE), 1)
    t = pl.program_id(1)

    @pl.when(t == 0)
    def _():
        _transpose_rows(vl_ref, vlt_s, SEQ)
        _transpose_rows(vc_ref, vct_s, CTX_LEN)

    def attend(with_band):
        if with_band:
            r0 = jnp.clip(NA_QROWS * t - NA_KH // 2, 0, GRID_ROWS - NA_BAND_ROWS)
            start = pl.multiple_of(r0 * GRID_W, NA_QBLK)

        def chunk(h, c):
            cols = slice((h // 2) * LANE, (h // 2 + 1) * LANE)
            rows = slice(h * NA_HEAD_DIM, (h + 1) * NA_HEAD_DIM)
            if c == 0:
                return kc_ref[:, cols], vct_s[rows, :], None
            off = (c - 1) * KEY_CHUNK
            keys = pl.ds(pl.multiple_of(start + off, KEY_CHUNK), KEY_CHUNK)
            return kl_ref[keys, cols], vlt_s[rows, keys], bias_ref[0, h, off:off + KEY_CHUNK, :]

        qs = []
        for h in range(NA_HEADS):
            head_lanes = lane < NA_HEAD_DIM if h % 2 == 0 else lane >= NA_HEAD_DIM
            qs.append(jnp.where(head_lanes, q_ref[:, (h // 2) * LANE:(h // 2 + 1) * LANE], jnp.zeros((), BF16)))
        outs = _attend_t(qs, 1 + (NA_BAND // KEY_CHUNK if with_band else 0), chunk)
        o_ref[...] = jnp.concatenate(outs, axis=0).T.astype(BF16)

    pl.when(t < n_lat_q)(lambda: attend(True))
    pl.when(t >= n_lat_q)(lambda: attend(False))


def _na(z, bias, batch, need_ctx):
    ntok = z.shape[0]
    n_lat_q = SEQ // TQ
    ctx_base = batch * SEQ // CTX_LEN
    qrow = _q_row_block(n_lat_q, ctx_base)
    w = NA_W
    bias_type = lambda b, t: (jnp.where(t == 0, 0, jnp.where(t >= n_lat_q - 1, 2, 1)), 0, 0, 0)
    return pl.pallas_call(
        functools.partial(_na_kernel, n_lat_q=n_lat_q),
        name="na_attn",
        grid=(batch, n_lat_q + (1 if need_ctx else 0)),
        in_specs=[
            pl.BlockSpec((TQ, w), lambda b, t: (qrow(b, t), Z_CQ // w)),
            pl.BlockSpec((SEQ, w), lambda b, t: (b, Z_CK // w)),
            pl.BlockSpec((CTX_LEN, w), lambda b, t: (ctx_base + b, Z_CK // w)),
            pl.BlockSpec((SEQ, w), lambda b, t: (b, Z_CV // w)),
            pl.BlockSpec((CTX_LEN, w), lambda b, t: (ctx_base + b, Z_CV // w)),
            pl.BlockSpec((1, NA_HEADS, NA_BAND, NA_QBLK), bias_type),
        ],
        out_specs=pl.BlockSpec((TQ, w), lambda b, t: (qrow(b, t), 0)),
        out_shape=jax.ShapeDtypeStruct((ntok, w), BF16),
        scratch_shapes=[pltpu.VMEM((w, SEQ), BF16), pltpu.VMEM((w, CTX_LEN), BF16)],
        compiler_params=_cparams(("parallel", "arbitrary")),
    )(z, z, z, z, z, bias)


def _merge_kernel(h_ref, ya_ref, ybl_ref, ybc_ref, yc_ref, ga_ref, gb_ref, gc_ref, mod_ref,
                  wa_ref, wb_ref, wc_ref, wo_ref, o_ref, *, n_lat_tiles):
    is_lat = pl.program_id(0) < n_lat_tiles
    yb = jnp.where(is_lat, ybl_ref[...], ybc_ref[...])
    m = (_sigmoid(ga_ref[...].astype(F32)) * _dot(ya_ref[...], wa_ref[...])
         + _sigmoid(gb_ref[...].astype(F32)) * _dot(yb, wb_ref[...])
         + _sigmoid(gc_ref[...].astype(F32)) * _dot(yc_ref[...], wc_ref[...]))
    o_ref[...] = h_ref[...] + mod_ref[0, 2:3, :] * _dot(m.astype(BF16), wo_ref[...])


def _merge(h, ya, yb_lat, yb_ctx, yc, z, mod, wa, wb, wc, wo, batch, need_ctx):
    ntok, d = h.shape
    tm = TM_TOK
    n_lat_tiles = batch * SEQ // tm
    n_tiles = ntok // tm if need_ctx else n_lat_tiles
    per_batch = SEQ // tm
    bw = wa.shape[0]
    const = lambda i: (0, 0)
    tok = lambda width: pl.BlockSpec((tm, width), lambda i: (i, 0))
    gate = lambda g: pl.BlockSpec((tm, d), lambda i: (i, Z_GATES // d + g))
    return pl.pallas_call(
        functools.partial(_merge_kernel, n_lat_tiles=n_lat_tiles),
        name="merge",
        grid=(n_tiles,),
        in_specs=[
            tok(d), tok(bw),
            pl.BlockSpec((tm, bw), lambda i: (jnp.minimum(i, n_lat_tiles - 1), 0)),
            pl.BlockSpec((tm, bw), lambda i: (jnp.maximum(i - n_lat_tiles, 0), 0)),
            tok(bw), gate(0), gate(1), gate(2),
            pl.BlockSpec((1, 6, d), lambda i: (jnp.where(i < n_lat_tiles, i // per_batch, batch), 0, 0)),
            pl.BlockSpec((bw, d), const), pl.BlockSpec((bw, d), const), pl.BlockSpec((bw, d), const),
            pl.BlockSpec((d, d), const),
        ],
        out_specs=tok(d),
        out_shape=jax.ShapeDtypeStruct((ntok, d), F32),
        input_output_aliases={0: 0},
        compiler_params=_cparams(("parallel",)),
    )(h, ya, yb_lat, yb_ctx, yc, z, z, z, mod, wa, wb, wc, wo)


def _ffn_kernel(h_ref, mod_ref, nw_ref, win_ref, wout_ref, fw_ref, o_ref, *, final):
    h = h_ref[...]
    u = (_rms(h, nw_ref[...]) * (1.0 + mod_ref[0, 4:5, :]) + mod_ref[0, 3:4, :]).astype(BF16)
    half = FFN_HIDDEN // 2
    acc = None
    for c in range(2):
        g = _dot(u, win_ref[:, c * half:(c + 1) * half])
        up = _dot(u, win_ref[:, FFN_HIDDEN + c * half:FFN_HIDDEN + (c + 1) * half])
        act = (g * _sigmoid(g) * up).astype(BF16)
        part = _dot(act, wout_ref[c * half:(c + 1) * half, :])
        acc = part if acc is None else acc + part
    out = h + mod_ref[0, 5:6, :] * acc
    if final:
        out = _rms(out, fw_ref[...])
    o_ref[...] = out


def _ffn(h, mod, norm_w, w_in, w_out, final_w, batch, need_ctx, final):
    ntok, d = h.shape
    tm = TM_TOK
    n_lat_tiles = batch * SEQ // tm
    n_tiles = ntok // tm if need_ctx else n_lat_tiles
    per_batch = SEQ // tm
    const = lambda i: (0, 0)
    resident = dict(pipeline_mode=pl.Buffered(1))
    out_rows = n_tiles * tm if final else ntok
    return pl.pallas_call(
        functools.partial(_ffn_kernel, final=final),
        name="ffn",
        grid=(n_tiles,),
        in_specs=[
            pl.BlockSpec((tm, d), lambda i: (i, 0)),
            pl.BlockSpec((1, 6, d), lambda i: (jnp.where(i < n_lat_tiles, i // per_batch, batch), 0, 0)),
            pl.BlockSpec((1, d), const),
            pl.BlockSpec((d, 2 * FFN_HIDDEN), const, **resident),
            pl.BlockSpec((FFN_HIDDEN, d), const, **resident),
            pl.BlockSpec((1, d), const),
        ],
        out_specs=pl.BlockSpec((tm, d), lambda i: (i, 0)),
        out_shape=jax.ShapeDtypeStruct((out_rows, d), F32),
        input_output_aliases={} if final else {0: 0},
        compiler_params=_cparams(("parallel",)),
    )(h, mod, norm_w, w_in, w_out, final_w)


def _pad_cols(w, width):
    return jnp.pad(w, ((0, 0), (0, 0), (0, width - w.shape[-1])))


def _prep_w_in(w_in):
    a0 = 0
    b0 = MLA_Q_RANK + MLA_KV_RANK + MLA_ROPE
    c0 = b0 + 2 * GLA_QK_W + 2 * GLA_V_W + 2 * GLA_GATE_RANK
    g0 = c0 + 3 * NA_W
    sl = lambda lo, n: w_in[:, :, lo:lo + n]
    aq = sl(a0, MLA_Q_RANK)
    akv = sl(a0 + MLA_Q_RANK, MLA_KV_RANK)
    kr0 = a0 + MLA_Q_RANK + MLA_KV_RANK
    half = MLA_ROPE // 2
    kr = sl(kr0, MLA_ROPE)
    kr_sw = jnp.concatenate([sl(kr0 + half, half), sl(kr0, half)], axis=-1)
    zeros = lambda n: jnp.zeros(w_in.shape[:2] + (n,), w_in.dtype)
    akr = jnp.concatenate([zeros(MLA_NOPE), kr, zeros(LANE - MLA_NOPE - MLA_ROPE),
                           zeros(MLA_NOPE), kr_sw, zeros(LANE - MLA_NOPE - MLA_ROPE)], axis=-1)
    bq = sl(b0, GLA_QK_W)
    bk = sl(b0 + GLA_QK_W, GLA_QK_W)
    bv = sl(b0 + 2 * GLA_QK_W, GLA_V_W)
    bg = sl(b0 + 2 * GLA_QK_W + GLA_V_W, GLA_V_W)
    lr = _pad_cols(sl(b0 + 2 * GLA_QK_W + 2 * GLA_V_W, 2 * GLA_GATE_RANK), LANE)
    cq = sl(c0, NA_W) * (NA_SCALE * LOG2E)
    ck = sl(c0 + NA_W, NA_W)
    cv = sl(c0 + 2 * NA_W, NA_W)
    gates = sl(g0, 3 * D_MODEL)
    out = jnp.concatenate([aq, lr, cq, ck, cv, bv, bg, bq, bk, akv, akr, gates], axis=-1)
    assert out.shape[-1] == Z_WIDTH
    return out.astype(BF16)


def _prep_mla_weights(w_q_up, w_kv_up):
    depth = w_q_up.shape[0]
    hd = MLA_NOPE + MLA_ROPE
    half = MLA_ROPE // 2
    wq = w_q_up.reshape(depth, MLA_Q_RANK, MLA_HEADS, hd)
    pad = jnp.zeros((depth, MLA_Q_RANK, MLA_HEADS, LANE - hd), w_q_up.dtype)
    plain = jnp.concatenate([wq, pad], axis=-1)
    swapped = jnp.concatenate([jnp.zeros_like(wq[..., :MLA_NOPE]), wq[..., MLA_NOPE + half:],
                               wq[..., MLA_NOPE:MLA_NOPE + half], pad], axis=-1)
    wq_p = jnp.concatenate([plain.reshape(depth, MLA_Q_RANK, -1), swapped.reshape(depth, MLA_Q_RANK, -1)], axis=-1)
    wkv = w_kv_up.reshape(depth, MLA_KV_RANK, MLA_HEADS, MLA_NOPE + MLA_V)
    wkn = jnp.concatenate([wkv[..., :MLA_NOPE],
                           jnp.zeros((depth, MLA_KV_RANK, MLA_HEADS, LANE - MLA_NOPE), w_kv_up.dtype)], axis=-1)
    wv = wkv[..., MLA_NOPE:]
    return (wq_p.astype(BF16), wkn.reshape(depth, MLA_KV_RANK, -1).astype(BF16),
            wv.reshape(depth, MLA_KV_RANK, -1).astype(BF16))


def _rope_tables():
    t = np.arange(SEQ)
    rows = (t // GRID_W).astype(np.float32)
    cols = (t % GRID_W).astype(np.float32)
    n_freq = MLA_ROPE // 4
    inv_freq = jnp.asarray(ROPE_BASE, F32) ** (-jnp.arange(n_freq, dtype=F32) / n_freq)
    ang = jnp.concatenate([jnp.asarray(rows)[:, None] * inv_freq, jnp.asarray(cols)[:, None] * inv_freq], axis=-1)
    cos, sin = jnp.cos(ang), jnp.sin(ang)
    cos = jnp.concatenate([cos, jnp.ones((TM_IN, MLA_ROPE // 2), F32)], axis=0)
    sin = jnp.concatenate([sin, jnp.zeros((TM_IN, MLA_ROPE // 2), F32)], axis=0)
    n = cos.shape[0]
    tail = jnp.zeros((n, LANE - MLA_NOPE - MLA_ROPE), F32)
    c_rot = jnp.concatenate([cos, cos], axis=-1)
    s_rot = jnp.concatenate([-sin, sin], axis=-1)
    nope1 = jnp.ones((n, MLA_NOPE), F32)
    nope0 = jnp.zeros((n, MLA_NOPE), F32)
    tqc = jnp.concatenate([nope1, c_rot, tail], axis=-1) * (MLA_SCALE * LOG2E)
    tqs = jnp.concatenate([nope0, s_rot, tail], axis=-1) * (MLA_SCALE * LOG2E)
    tkc = jnp.concatenate([nope0, c_rot, tail], axis=-1)
    tks = jnp.concatenate([nope0, s_rot, tail], axis=-1)
    return tqc, tqs, tkc, tks


def _na_bias_tables(rpb):
    depth = rpb.shape[0]
    n_off = 2 * NA_KH - 1
    w2 = 2 * GRID_W
    lo = GRID_W - NA_KW
    u = jnp.pad(rpb.astype(F32), ((0, 0), (0, 0), (0, 0), (lo, w2 - lo - (2 * NA_KW - 1))))
    skew = jnp.tile(u, (1, 1, 1, GRID_W))[..., :GRID_W * (w2 - 1)].reshape(depth, NA_HEADS, n_off, GRID_W, w2 - 1)
    toep = skew[..., GRID_W - 1:]
    col = np.arange(GRID_W)
    cs = np.clip(col - NA_KW // 2, 0, GRID_W - NA_KW)
    col_ok = (col[None, :] >= cs[:, None]) & (col[None, :] < cs[:, None] + NA_KW)
    toep = jnp.swapaxes(jnp.where(jnp.asarray(col_ok), toep * LOG2E, -1e30), -1, -2)
    masked = jnp.full((depth, NA_HEADS, GRID_W, GRID_W), -1e30, F32)
    n_blocks = GRID_ROWS // NA_QROWS
    tables = []
    for g in (0, 1, n_blocks - 1):
        band0 = int(np.clip(NA_QROWS * g - NA_KH // 2, 0, GRID_ROWS - NA_BAND_ROWS))
        k_rows = []
        for i in range(NA_BAND_ROWS):
            kr = band0 + i
            blocks = []
            for j in range(NA_QROWS):
                qr = NA_QROWS * g + j
                r0 = int(np.clip(qr - NA_KH // 2, 0, GRID_ROWS - NA_KH))
                blocks.append(toep[:, :, kr - qr + NA_KH - 1] if r0 <= kr < r0 + NA_KH else masked)
            k_rows.append(jnp.concatenate(blocks, axis=-1))
        tables.append(jnp.concatenate(k_rows, axis=-2))
    return jnp.stack(tables, axis=1)


def _prep_gla_gate(w_f, b_f, w_b, b_b):
    depth = w_f.shape[0]
    wg = jnp.zeros((depth, LANE, 2 * GLA_QK_W), F32)
    wg = wg.at[:, :GLA_GATE_RANK, :GLA_QK_W].set(w_f)
    wg = wg.at[:, GLA_GATE_RANK:2 * GLA_GATE_RANK, GLA_QK_W:].set(w_b)
    bg = jnp.concatenate([b_f, b_b], axis=-1)[:, None, :]
    return wg.astype(BF16), bg


def kernel(x, c, ctx, c_ctx, w_mod, b_mod, norm1_w, w_in, mla_q_norm_w, mla_kv_norm_w, mla_w_q_up, mla_w_kv_up, gla_w_gate_f, gla_b_gate_f, gla_w_gate_b, gla_b_gate_b, gla_norm_w, na_rpb, w_a_o, w_b_o, w_c_o, w_out, norm2_w, w_ffn_in, w_ffn_out, final_norm_w):
    batch, seq, d = x.shape
    depth = w_mod.shape[0]
    assert seq == SEQ and d == D_MODEL and ctx.shape[1] == CTX_LEN
    assert (batch * CTX_LEN) % TM_IN == 0
    n_lat_in = batch * SEQ // TM_IN

    mod_rows = -(-(batch + 1) // 8) * 8
    c_all = jnp.concatenate([c, c_ctx[None, :], jnp.zeros((mod_rows - batch - 1, d), F32)], axis=0)
    mod_all = _modulation(c_all, w_mod, b_mod).reshape(depth, mod_rows, 6, d)

    w_in_p = _prep_w_in(w_in)
    wq_p, wkn_p, wv_p = _prep_mla_weights(mla_w_q_up, mla_w_kv_up)
    tabs = _rope_tables()
    na_bias = _na_bias_tables(na_rpb)
    wg_p, bg_p = _prep_gla_gate(gla_w_gate_f, gla_b_gate_f, gla_w_gate_b, gla_b_gate_b)
    wa, wb, wc, wo = (w.astype(BF16) for w in (w_a_o, w_b_o, w_c_o, w_out))
    wfi, wfo = w_ffn_in.astype(BF16), w_ffn_out.astype(BF16)
    row = lambda w, i: w[i][None, :]

    h = jnp.concatenate([x.reshape(batch * SEQ, d), ctx.reshape(batch * CTX_LEN, d)], axis=0)
    for i in range(depth):
        need_ctx = i < depth - 1
        mod = mod_all[i]
        z = _inproj(h, mod, row(norm1_w, i), w_in_p[i], n_lat_in, batch)
        q, k, v = _mla_prep(z, row(mla_q_norm_w, i), row(mla_kv_norm_w, i), wq_p[i], wkn_p[i], wv_p[i], tabs, n_lat_in)
        ya = _mla_attn(q, k, v, batch, need_ctx)
        yb_lat, yb_ctx = _gla(z, wg_p[i], bg_p[i], row(gla_norm_w, i), batch)
        yc = _na(z, na_bias[i], batch, need_ctx)
        h = _merge(h, ya, yb_lat, yb_ctx, yc, z, mod, wa[i], wb[i], wc[i], wo[i], batch, need_ctx)
        h = _ffn(h, mod, row(norm2_w, i), wfi[i], wfo[i], final_norm_w[None, :], batch, need_ctx, not need_ctx)
    return h.reshape(batch, SEQ, d)
```

```python
import functools

import jax
import jax.numpy as jnp
import numpy as np
from jax import lax
from jax.experimental import pallas as pl
from jax.experimental.pallas import tpu as pltpu

F32 = jnp.float32
BF16 = jnp.bfloat16

D_MODEL = 1024
SEQ = 2048
CTX_LEN = 256
GRID_W = 64
GRID_ROWS = SEQ // GRID_W
EPS = 1e-6
ROPE_BASE = 10000.0

MLA_HEADS = 8
MLA_Q_RANK = 384
MLA_KV_RANK = 256
MLA_NOPE = 64
MLA_ROPE = 32
MLA_V = 64
MLA_SCALE = (MLA_NOPE + MLA_ROPE) ** -0.5

GLA_HEADS = 4
GLA_DK = 64
GLA_DV = 128
GLA_GATE_RANK = 16
GLA_GATE_TAU = 16.0
GLA_CHUNK = 64
GLA_TILE = 256
GLA_PAIR = 2
GLA_QK_W = GLA_HEADS * GLA_DK
GLA_V_W = GLA_HEADS * GLA_DV

NA_HEADS = 8
NA_HEAD_DIM = 64
NA_KH = 8
NA_KW = 16
NA_W = NA_HEADS * NA_HEAD_DIM
NA_SCALE = NA_HEAD_DIM ** -0.5
LOG2E = 1.4426950408889634
NA_QROWS = 4
NA_BAND_ROWS = NA_KH + NA_QROWS
NA_BAND = NA_BAND_ROWS * GRID_W
NA_QBLK = NA_QROWS * GRID_W

FFN_HIDDEN = 2816
LANE = 128
BF16_SUBLANES = 16

Z_AQ = 0
Z_LR = 384
Z_CQ = 512
Z_CK = 1024
Z_CV = 1536
Z_BV = 2048
Z_BG = 2560
Z_BQ = 3072
Z_BK = 3328
Z_AKV = 3584
Z_AKR = 3840
Z_GATES = 4096
Z_WIDTH = 7168

TM_IN = 1024
TN_IN = 1024
TM_TOK = 512
TQ = 256
KEY_CHUNK = 256
MLA_KEY_CHUNK = 512
HEAD_GROUP = 8
VMEM_LIMIT = 56 * 1024 * 1024


def _cparams(sem):
    return pltpu.CompilerParams(dimension_semantics=sem, vmem_limit_bytes=VMEM_LIMIT)


def _dot(a, b):
    return jnp.dot(a, b, preferred_element_type=F32)


def _dot_nt(a, b):
    return lax.dot_general(a, b, (((1,), (1,)), ((), ())), preferred_element_type=F32)


def _dot_tn(a, b):
    return lax.dot_general(a, b, (((0,), (0,)), ((), ())), preferred_element_type=F32)


def _sigmoid(x):
    return 1.0 / (1.0 + jnp.exp(-x))


def _rms(x, w):
    return x * lax.rsqrt(jnp.mean(x * x, axis=-1, keepdims=True) + EPS) * w


def _split3(x):
    hi = x.astype(BF16)
    r1 = x - hi.astype(F32)
    mid = r1.astype(BF16)
    lo = (r1 - mid.astype(F32)).astype(BF16)
    return hi, mid, lo


def _mod_kernel(c_ref, w_ref, b_ref, o_ref):
    c = c_ref[...]
    a = c * _sigmoid(c)
    a_hi = a.astype(BF16)
    a_lo = (a - a_hi.astype(F32)).astype(BF16)
    w = w_ref[0]
    w_hi = w.astype(BF16)
    w_lo = (w - w_hi.astype(F32)).astype(BF16)
    o_ref[0] = _dot(a_hi, w_hi) + _dot(a_lo, w_hi) + _dot(a_hi, w_lo) + b_ref[0]


def _modulation(c_all, w_mod, b_mod):
    depth, d, n = w_mod.shape
    rows = c_all.shape[0]
    tn = 1536
    return pl.pallas_call(
        _mod_kernel,
        name="adaln_mod",
        grid=(depth, n // tn),
        in_specs=[
            pl.BlockSpec((rows, d), lambda l, j: (0, 0)),
            pl.BlockSpec((1, d, tn), lambda l, j: (l, 0, j)),
            pl.BlockSpec((1, 1, tn), lambda l, j: (l, 0, j)),
        ],
        out_specs=pl.BlockSpec((1, rows, tn), lambda l, j: (l, 0, j)),
        out_shape=jax.ShapeDtypeStruct((depth, rows, n), F32),
        compiler_params=_cparams(("parallel", "parallel")),
    )(c_all, w_mod, b_mod.reshape(depth, 1, n))


def _inproj_kernel(h_ref, mod_ref, nw_ref, w_ref, o_ref):
    y = _rms(h_ref[...], nw_ref[...])
    xn = (y * (1.0 + mod_ref[0, 1:2, :]) + mod_ref[0, 0:1, :]).astype(BF16)
    for c in range(0, Z_WIDTH, TN_IN):
        o_ref[:, c:c + TN_IN] = _dot(xn, w_ref[:, c:c + TN_IN]).astype(BF16)


def _inproj(h, mod, norm_w, w_in_p, batch):
    ntok, d = h.shape
    tm = TM_TOK
    n_lat_tiles = batch * SEQ // tm
    per_batch = SEQ // tm
    const = lambda i: (0, 0)
    return pl.pallas_call(
        _inproj_kernel,
        name="in_proj",
        grid=(ntok // tm,),
        in_specs=[
            pl.BlockSpec((tm, d), lambda i: (i, 0)),
            pl.BlockSpec((1, 6, d), lambda i: (jnp.where(i < n_lat_tiles, i // per_batch, batch), 0, 0)),
            pl.BlockSpec((1, d), const),
            pl.BlockSpec((d, Z_WIDTH), const, pipeline_mode=pl.Buffered(1)),
        ],
        out_specs=pl.BlockSpec((tm, Z_WIDTH), lambda i: (i, 0)),
        out_shape=jax.ShapeDtypeStruct((ntok, Z_WIDTH), BF16),
        compiler_params=_cparams(("parallel",)),
    )(h, mod, norm_w, w_in_p)


def _mla_prep_kernel(aq_ref, akv_ref, akr_ref, qnw_ref, kvnw_ref, wq_ref, wkn_ref, wv_ref,
                     tqc_ref, tqs_ref, tkc_ref, tks_ref, q_out, k_out, v_out):
    qn = _rms(aq_ref[...].astype(F32), qnw_ref[...]).astype(BF16)
    qq = _dot(qn, wq_ref[...])
    hw = MLA_HEADS * LANE
    tqc = tqc_ref[...]
    tqs = tqs_ref[...]
    for h in range(MLA_HEADS):
        lo = h * LANE
        q_out[:, lo:lo + LANE] = (qq[:, lo:lo + LANE] * tqc + qq[:, hw + lo:hw + lo + LANE] * tqs).astype(BF16)
    kvn = _rms(akv_ref[...].astype(F32), kvnw_ref[...]).astype(BF16)
    kn = _dot(kvn, wkn_ref[...])
    r = akr_ref[...].astype(F32)
    kr = r[:, :LANE] * tkc_ref[...] + r[:, LANE:] * tks_ref[...]
    for h in range(MLA_HEADS):
        lo = h * LANE
        k_out[:, lo:lo + LANE] = (kn[:, lo:lo + LANE] + kr).astype(BF16)
    v_out[...] = _dot(kvn, wv_ref[...]).astype(BF16)


def _mla_prep(z, qnw, kvnw, wq, wkn, wv, tabs, n_lat_tiles):
    ntok = z.shape[0]
    tm = TM_IN
    per_batch = SEQ // tm
    tab_idx = lambda i: (jnp.where(i < n_lat_tiles, i % per_batch, per_batch), 0)
    const = lambda i: (0, 0)
    hw = MLA_HEADS * LANE
    tab_spec = pl.BlockSpec((tm, LANE), tab_idx)
    return pl.pallas_call(
        _mla_prep_kernel,
        name="mla_prep",
        grid=(ntok // tm,),
        in_specs=[
            pl.BlockSpec((tm, MLA_Q_RANK), lambda i: (i, Z_AQ // MLA_Q_RANK)),
            pl.BlockSpec((tm, MLA_KV_RANK), lambda i: (i, Z_AKV // MLA_KV_RANK)),
            pl.BlockSpec((tm, 2 * LANE), lambda i: (i, Z_AKR // (2 * LANE))),
            pl.BlockSpec((1, MLA_Q_RANK), const),
            pl.BlockSpec((1, MLA_KV_RANK), const),
            pl.BlockSpec((MLA_Q_RANK, 2 * hw), const),
            pl.BlockSpec((MLA_KV_RANK, hw), const),
            pl.BlockSpec((MLA_KV_RANK, MLA_HEADS * MLA_V), const),
            tab_spec, tab_spec, tab_spec, tab_spec,
        ],
        out_specs=[
            pl.BlockSpec((tm, hw), lambda i: (i, 0)),
            pl.BlockSpec((tm, hw), lambda i: (i, 0)),
            pl.BlockSpec((tm, MLA_HEADS * MLA_V), lambda i: (i, 0)),
        ],
        out_shape=[
            jax.ShapeDtypeStruct((ntok, hw), BF16),
            jax.ShapeDtypeStruct((ntok, hw), BF16),
            jax.ShapeDtypeStruct((ntok, MLA_HEADS * MLA_V), BF16),
        ],
        compiler_params=_cparams(("parallel",)),
    )(z, z, z, qnw, kvnw, wq, wkn, wv, *tabs)


def _attend_t(qs, n_chunks, chunk):
    if len(qs) > HEAD_GROUP:
        outs = []
        for g in range(0, len(qs), HEAD_GROUP):
            outs += _attend_t(qs[g:g + HEAD_GROUP], n_chunks, lambda h, c, g=g: chunk(g + h, c))
        return outs
    heads = range(len(qs))
    dv = chunk(0, 0)[1].shape[0]
    m = o = None
    for c in range(n_chunks):
        parts = [chunk(h, c) for h in heads]
        s = [_dot_nt(parts[h][0], qs[h]) for h in heads]
        s = [s[h] if parts[h][2] is None else s[h] + parts[h][2] for h in heads]
        mc = [jnp.max(s[h], axis=0, keepdims=True) for h in heads]
        ones = jnp.ones((BF16_SUBLANES, parts[0][1].shape[1]), BF16)
        vt1 = [jnp.concatenate([parts[h][1], ones], axis=0) for h in heads]
        if m is None:
            m = mc
            o = [_dot(vt1[h], jnp.exp2(s[h] - m[h]).astype(BF16)) for h in heads]
        else:
            m_new = [jnp.maximum(m[h], mc[h]) for h in heads]
            alpha = [jnp.exp2(m[h] - m_new[h]) for h in heads]
            o = [o[h] * alpha[h] + _dot(vt1[h], jnp.exp2(s[h] - m_new[h]).astype(BF16)) for h in heads]
            m = m_new
    return [o[h][:dv] / o[h][dv:dv + 1] for h in heads]


def _transpose_rows(src_ref, dst_ref, rows):
    for r in range(0, rows, TQ):
        dst_ref[:, r:r + TQ] = src_ref[r:r + TQ, :].astype(F32).T.astype(BF16)


def _mla_attn_kernel(q_ref, kl_ref, kc_ref, vl_ref, vc_ref, o_ref, vlt_s, vct_s, *, n_lat_q):
    t = pl.program_id(1)

    @pl.when(t == 0)
    def _():
        _transpose_rows(vl_ref, vlt_s, SEQ)
        _transpose_rows(vc_ref, vct_s, CTX_LEN)

    def attend(with_lat):
        def chunk(h, c):
            cols = slice(h * LANE, (h + 1) * LANE)
            rows = slice(h * MLA_V, (h + 1) * MLA_V)
            if c == 0:
                return kc_ref[:, cols], vct_s[rows, :], None
            keys = slice((c - 1) * MLA_KEY_CHUNK, c * MLA_KEY_CHUNK)
            return kl_ref[keys, cols], vlt_s[rows, keys], None

        qs = [q_ref[:, h * LANE:(h + 1) * LANE] for h in range(MLA_HEADS)]
        outs = _attend_t(qs, 1 + (SEQ // MLA_KEY_CHUNK if with_lat else 0), chunk)
        o_ref[...] = jnp.concatenate(outs, axis=0).T.astype(BF16)

    pl.when(t < n_lat_q)(lambda: attend(True))
    pl.when(t >= n_lat_q)(lambda: attend(False))


def _q_row_block(n_lat_q, ctx_base):
    return lambda b, t: jnp.where(t < n_lat_q, b * n_lat_q + t, ctx_base + b)


def _mla_attn(q, k, v, batch, need_ctx):
    ntok = q.shape[0]
    n_lat_q = SEQ // TQ
    ctx_base = batch * SEQ // CTX_LEN
    qrow = _q_row_block(n_lat_q, ctx_base)
    hw = MLA_HEADS * LANE
    vw = MLA_HEADS * MLA_V
    return pl.pallas_call(
        functools.partial(_mla_attn_kernel, n_lat_q=n_lat_q),
        name="mla_attn",
        grid=(batch, n_lat_q + (1 if need_ctx else 0)),
        in_specs=[
            pl.BlockSpec((TQ, hw), lambda b, t: (qrow(b, t), 0)),
            pl.BlockSpec((SEQ, hw), lambda b, t: (b, 0)),
            pl.BlockSpec((CTX_LEN, hw), lambda b, t: (ctx_base + b, 0)),
            pl.BlockSpec((SEQ, vw), lambda b, t: (b, 0)),
            pl.BlockSpec((CTX_LEN, vw), lambda b, t: (ctx_base + b, 0)),
        ],
        out_specs=pl.BlockSpec((TQ, vw), lambda b, t: (qrow(b, t), 0)),
        out_shape=jax.ShapeDtypeStruct((ntok, vw), BF16),
        scratch_shapes=[pltpu.VMEM((vw, SEQ), BF16), pltpu.VMEM((vw, CTX_LEN), BF16)],
        compiler_params=_cparams(("parallel", "arbitrary")),
    )(q, k, k, v, v)


def _gla_kernel(ql_ref, qc_ref, kl_ref, kc_ref, vl_ref, vc_ref, gl_ref, gc_ref, lrl_ref, lrc_ref,
                wg_ref, bg_ref, nw_ref, yl_ref, yc_ref,
                v_s, qd_s, ke_s, dec_s, o_s, st_s):
    n_tok = SEQ + CTX_LEN
    n_chunk = n_tok // GLA_CHUNK
    ctx_chunks = CTX_LEN // GLA_CHUNK
    ck = GLA_CHUNK
    tile = GLA_TILE
    per_tile = tile // ck
    qk_w = GLA_QK_W
    pair_k = GLA_PAIR * GLA_DK
    pair_v = GLA_PAIR * GLA_DV
    n_pairs = GLA_HEADS // GLA_PAIR

    v_s[0:SEQ, :] = vl_ref[...]
    v_s[SEQ:n_tok, :] = vc_ref[...]
    st_s[...] = jnp.zeros_like(st_s)

    row = lax.broadcasted_iota(jnp.int32, (tile, tile), 0)
    col = lax.broadcasted_iota(jnp.int32, (tile, tile), 1)
    same_chunk = (row // ck) == (col // ck)
    keep = (same_chunk & (col <= row), same_chunk & (col >= row))
    tri3 = tuple(jnp.concatenate([jnp.where(kp, 1.0, 0.0).astype(BF16)] * 3, axis=1) for kp in keep)
    q_head = lax.broadcasted_iota(jnp.int32, (tile, qk_w), 1) // GLA_DK
    bd_pair = (lax.broadcasted_iota(jnp.int32, (pair_k, pair_v), 0) // GLA_DK
               == lax.broadcasted_iota(jnp.int32, (pair_k, pair_v), 1) // GLA_DV)

    def log_decay(lr):
        zz = _dot(lr, wg_ref[...]) + bg_ref[...]
        return (jnp.minimum(zz, 0.0) - jnp.log1p(jnp.exp(-jnp.abs(zz)))) * (1.0 / GLA_GATE_TAU)

    def prologue(q, k, v, lr, r0):
        rows = pl.ds(r0, tile)
        q = q.astype(F32) * (GLA_DK ** -0.5)
        k = k.astype(F32)
        la = log_decay(lr)
        dirs = range(2)
        heads = range(GLA_HEADS)
        cum = [_dot(tri3[d], jnp.concatenate(_split3(la[:, d * qk_w:(d + 1) * qk_w]), axis=0)) for d in dirs]
        edge = (ck - 1, 0)
        lasts = [[cum[d][c * ck + edge[d]:c * ck + edge[d] + 1, :] for c in range(per_tile)] for d in dirs]
        last = [jnp.concatenate([jnp.broadcast_to(x, (ck, qk_w)) for x in lasts[d]], axis=0) for d in dirs]
        q_dec = [(q * jnp.exp(cum[d])).astype(BF16) for d in dirs]
        k_inv = [(k * jnp.exp(-cum[d])).astype(BF16) for d in dirs]
        for d in dirs:
            qd_s[d, rows, :] = q_dec[d]
            ke_s[d, rows, :] = (k * jnp.exp(last[d] - cum[d])).astype(BF16)
            for c in range(per_tile):
                dec_s[d, r0 // ck + c] = jnp.broadcast_to(jnp.exp(lasts[d][c]), (8, qk_w))
        qm = [[jnp.where(q_head == h, q_dec[d], jnp.zeros((), BF16)) for h in heads] for d in dirs]
        s = [[_dot_nt(qm[d][h], k_inv[d]) for h in heads] for d in dirs]
        att = [[jnp.where(keep[d], s[d][h], 0.0).astype(BF16) for h in heads] for d in dirs]
        o = [[_dot(att[d][h], v[:, h * GLA_DV:(h + 1) * GLA_DV]) for h in heads] for d in dirs]
        for d in dirs:
            for h in heads:
                o_s[d, rows, h * GLA_DV:(h + 1) * GLA_DV] = o[d][h]

    def lat_tile(i, carry):
        r0 = pl.multiple_of(i * tile, tile)
        rows = pl.ds(r0, tile)
        prologue(ql_ref[rows, :], kl_ref[rows, :], vl_ref[rows, :], lrl_ref[rows, :], r0)
        return carry

    lax.fori_loop(0, SEQ // tile, lat_tile, 0)
    for r0 in range(0, CTX_LEN, tile):
        rows = slice(r0, r0 + tile)
        prologue(qc_ref[rows, :], kc_ref[rows, :], vc_ref[rows, :], lrc_ref[rows, :], SEQ + r0)

    def scan_step(n, d):
        rows = pl.ds(pl.multiple_of(n * ck, ck), ck)
        dec = dec_s[d, n]
        for p in range(n_pairs):
            kcols = slice(p * pair_k, (p + 1) * pair_k)
            vcols = slice(p * pair_v, (p + 1) * pair_v)
            st = st_s[d, p]
            o_s[d, rows, vcols] = o_s[d, rows, vcols] + _dot(qd_s[d, rows, kcols], st.astype(BF16))
            kv = _dot_tn(ke_s[d, rows, kcols], v_s[rows, vcols])
            decay_rows = jnp.broadcast_to(dec[0:1, kcols], (pair_v, pair_k)).T
            st_s[d, p] = st * decay_rows + jnp.where(bd_pair, kv, 0.0)

    def body(i, carry):
        nf = jnp.where(i < ctx_chunks, n_chunk - ctx_chunks + i, i - ctx_chunks)
        nb = n_chunk - 1 - i
        scan_step(nf, 0)
        scan_step(nb, 1)
        return carry

    lax.fori_loop(0, n_chunk, body, 0, unroll=2)

    nw = nw_ref[...]

    def finish(r0, rows, g):
        o = o_s[0, r0:r0 + rows, :] + o_s[1, r0:r0 + rows, :]
        g = g.astype(F32)
        parts = []
        for h in range(GLA_HEADS):
            parts.append(_rms(o[:, h * GLA_DV:(h + 1) * GLA_DV], nw))
        y = jnp.concatenate(parts, axis=1)
        return (y * (g * _sigmoid(g))).astype(BF16)

    rt = GLA_TILE
    for i in range(SEQ // rt):
        yl_ref[i * rt:(i + 1) * rt, :] = finish(i * rt, rt, gl_ref[i * rt:(i + 1) * rt, :])
    yc_ref[...] = finish(SEQ, CTX_LEN, gc_ref[...])


def _gla(z, wg, bg, nw, batch):
    ctx_base = batch * SEQ // CTX_LEN
    n_tok = SEQ + CTX_LEN

    def lat(width, col0):
        return pl.BlockSpec((SEQ, width), lambda b: (b, col0 // width))

    def ctx(width, col0):
        return pl.BlockSpec((CTX_LEN, width), lambda b: (ctx_base + b, col0 // width))

    const = lambda b: (0, 0)
    return pl.pallas_call(
        _gla_kernel,
        name="gla_scan",
        grid=(batch,),
        in_specs=[
            lat(GLA_QK_W, Z_BQ), ctx(GLA_QK_W, Z_BQ),
            lat(GLA_QK_W, Z_BK), ctx(GLA_QK_W, Z_BK),
            lat(GLA_V_W, Z_BV), ctx(GLA_V_W, Z_BV),
            lat(GLA_V_W, Z_BG), ctx(GLA_V_W, Z_BG),
            lat(LANE, Z_LR), ctx(LANE, Z_LR),
            pl.BlockSpec((LANE, 2 * GLA_QK_W), const),
            pl.BlockSpec((1, 2 * GLA_QK_W), const),
            pl.BlockSpec((1, GLA_DV), const),
        ],
        out_specs=[
            pl.BlockSpec((SEQ, GLA_V_W), lambda b: (b, 0)),
            pl.BlockSpec((CTX_LEN, GLA_V_W), lambda b: (b, 0)),
        ],
        out_shape=[
            jax.ShapeDtypeStruct((batch * SEQ, GLA_V_W), BF16),
            jax.ShapeDtypeStruct((batch * CTX_LEN, GLA_V_W), BF16),
        ],
        scratch_shapes=[
            pltpu.VMEM((n_tok, GLA_V_W), BF16),
            pltpu.VMEM((2, n_tok, GLA_QK_W), BF16),
            pltpu.VMEM((2, n_tok, GLA_QK_W), BF16),
            pltpu.VMEM((2, n_tok // GLA_CHUNK, 8, GLA_QK_W), F32),
            pltpu.VMEM((2, n_tok, GLA_V_W), F32),
            pltpu.VMEM((2, GLA_HEADS // GLA_PAIR, GLA_PAIR * GLA_DK, GLA_PAIR * GLA_DV), F32),
        ],
        compiler_params=_cparams(("parallel",)),
    )(z, z, z, z, z, z, z, z, z, z, wg, bg, nw)


def _na_kernel(q_ref, kl_ref, kc_ref, vl_ref, vc_ref, bias_ref, o_ref, vlt_s, vct_s, *, n_lat_q):
    lane = lax.broadcasted_iota(jnp.int32, (TQ, LANE), 1)
    t = pl.program_id(1)

    @pl.when(t == 0)
    def _():
        _transpose_rows(vl_ref, vlt_s, SEQ)
        _transpose_rows(vc_ref, vct_s, CTX_LEN)

    def attend(with_band):
        if with_band:
            r0 = jnp.clip(NA_QROWS * t - NA_KH // 2, 0, GRID_ROWS - NA_BAND_ROWS)
            start = pl.multiple_of(r0 * GRID_W, NA_QBLK)
            kind = jnp.where(t == 0, 0, jnp.where(t == n_lat_q - 1, 2, 1))

        def chunk(h, c):
            cols = slice((h // 2) * LANE, (h // 2 + 1) * LANE)
            rows = slice(h * NA_HEAD_DIM, (h + 1) * NA_HEAD_DIM)
            if c == 0:
                return kc_ref[:, cols], vct_s[rows, :], None
            off = (c - 1) * KEY_CHUNK
            keys = pl.ds(pl.multiple_of(start + off, KEY_CHUNK), KEY_CHUNK)
            return kl_ref[keys, cols], vlt_s[rows, keys], bias_ref[kind, h, off:off + KEY_CHUNK, :]

        qs = []
        for h in range(NA_HEADS):
            head_lanes = lane < NA_HEAD_DIM if h % 2 == 0 else lane >= NA_HEAD_DIM
            qs.append(jnp.where(head_lanes, q_ref[:, (h // 2) * LANE:(h // 2 + 1) * LANE], jnp.zeros((), BF16)))
        outs = _attend_t(qs, 1 + (NA_BAND // KEY_CHUNK if with_band else 0), chunk)
        o_ref[...] = jnp.concatenate(outs, axis=0).T.astype(BF16)

    pl.when(t < n_lat_q)(lambda: attend(True))
    pl.when(t >= n_lat_q)(lambda: attend(False))


def _na(z, bias, batch, need_ctx):
    ntok = z.shape[0]
    n_lat_q = SEQ // TQ
    ctx_base = batch * SEQ // CTX_LEN
    qrow = _q_row_block(n_lat_q, ctx_base)
    w = NA_W
    return pl.pallas_call(
        functools.partial(_na_kernel, n_lat_q=n_lat_q),
        name="na_attn",
        grid=(batch, n_lat_q + (1 if need_ctx else 0)),
        in_specs=[
            pl.BlockSpec((TQ, w), lambda b, t: (qrow(b, t), Z_CQ // w)),
            pl.BlockSpec((SEQ, w), lambda b, t: (b, Z_CK // w)),
            pl.BlockSpec((CTX_LEN, w), lambda b, t: (ctx_base + b, Z_CK // w)),
            pl.BlockSpec((SEQ, w), lambda b, t: (b, Z_CV // w)),
            pl.BlockSpec((CTX_LEN, w), lambda b, t: (ctx_base + b, Z_CV // w)),
            pl.BlockSpec((3, NA_HEADS, NA_BAND, NA_QBLK), lambda b, t: (0, 0, 0, 0), pipeline_mode=pl.Buffered(1)),
        ],
        out_specs=pl.BlockSpec((TQ, w), lambda b, t: (qrow(b, t), 0)),
        out_shape=jax.ShapeDtypeStruct((ntok, w), BF16),
        scratch_shapes=[pltpu.VMEM((w, SEQ), BF16), pltpu.VMEM((w, CTX_LEN), BF16)],
        compiler_params=_cparams(("parallel", "arbitrary")),
    )(z, z, z, z, z, bias)


def _merge_kernel(h_ref, ya_ref, ybl_ref, ybc_ref, yc_ref, ga_ref, gb_ref, gc_ref, mod_ref,
                  wa_ref, wb_ref, wc_ref, wo_ref, o_ref, *, n_lat_tiles):
    is_lat = pl.program_id(0) < n_lat_tiles
    yb = jnp.where(is_lat, ybl_ref[...], ybc_ref[...])
    m = (_sigmoid(ga_ref[...].astype(F32)) * _dot(ya_ref[...], wa_ref[...])
         + _sigmoid(gb_ref[...].astype(F32)) * _dot(yb, wb_ref[...])
         + _sigmoid(gc_ref[...].astype(F32)) * _dot(yc_ref[...], wc_ref[...]))
    o_ref[...] = h_ref[...] + mod_ref[0, 2:3, :] * _dot(m.astype(BF16), wo_ref[...])


def _merge(h, ya, yb_lat, yb_ctx, yc, z, mod, wa, wb, wc, wo, batch, need_ctx):
    ntok, d = h.shape
    tm = TM_TOK
    n_lat_tiles = batch * SEQ // tm
    n_tiles = ntok // tm if need_ctx else n_lat_tiles
    per_batch = SEQ // tm
    bw = wa.shape[0]
    const = lambda i: (0, 0)
    tok = lambda width: pl.BlockSpec((tm, width), lambda i: (i, 0))
    gate = lambda g: pl.BlockSpec((tm, d), lambda i: (i, Z_GATES // d + g))
    return pl.pallas_call(
        functools.partial(_merge_kernel, n_lat_tiles=n_lat_tiles),
        name="merge",
        grid=(n_tiles,),
        in_specs=[
            tok(d), tok(bw),
            pl.BlockSpec((tm, bw), lambda i: (jnp.minimum(i, n_lat_tiles - 1), 0)),
            pl.BlockSpec((tm, bw), lambda i: (jnp.maximum(i - n_lat_tiles, 0), 0)),
            tok(bw), gate(0), gate(1), gate(2),
            pl.BlockSpec((1, 6, d), lambda i: (jnp.where(i < n_lat_tiles, i // per_batch, batch), 0, 0)),
            pl.BlockSpec((bw, d), const), pl.BlockSpec((bw, d), const), pl.BlockSpec((bw, d), const),
            pl.BlockSpec((d, d), const),
        ],
        out_specs=tok(d),
        out_shape=jax.ShapeDtypeStruct((ntok, d), F32),
        input_output_aliases={0: 0},
        compiler_params=_cparams(("parallel",)),
    )(h, ya, yb_lat, yb_ctx, yc, z, z, z, mod, wa, wb, wc, wo)


def _ffn_kernel(h_ref, mod_ref, nw_ref, win_ref, wout_ref, fw_ref, o_ref, *, final):
    h = h_ref[...]
    u = (_rms(h, nw_ref[...]) * (1.0 + mod_ref[0, 4:5, :]) + mod_ref[0, 3:4, :]).astype(BF16)
    half = FFN_HIDDEN // 2
    acc = None
    for c in range(2):
        g = _dot(u, win_ref[:, c * half:(c + 1) * half])
        up = _dot(u, win_ref[:, FFN_HIDDEN + c * half:FFN_HIDDEN + (c + 1) * half])
        act = (g * _sigmoid(g) * up).astype(BF16)
        part = _dot(act, wout_ref[c * half:(c + 1) * half, :])
        acc = part if acc is None else acc + part
    out = h + mod_ref[0, 5:6, :] * acc
    if final:
        out = _rms(out, fw_ref[...])
    o_ref[...] = out


def _ffn(h, mod, norm_w, w_in, w_out, final_w, batch, need_ctx, final):
    ntok, d = h.shape
    tm = TM_TOK
    n_lat_tiles = batch * SEQ // tm
    n_tiles = ntok // tm if need_ctx else n_lat_tiles
    per_batch = SEQ // tm
    const = lambda i: (0, 0)
    resident = dict(pipeline_mode=pl.Buffered(1))
    out_rows = n_tiles * tm if final else ntok
    return pl.pallas_call(
        functools.partial(_ffn_kernel, final=final),
        name="ffn",
        grid=(n_tiles,),
        in_specs=[
            pl.BlockSpec((tm, d), lambda i: (i, 0)),
            pl.BlockSpec((1, 6, d), lambda i: (jnp.where(i < n_lat_tiles, i // per_batch, batch), 0, 0)),
            pl.BlockSpec((1, d), const),
            pl.BlockSpec((d, 2 * FFN_HIDDEN), const, **resident),
            pl.BlockSpec((FFN_HIDDEN, d), const, **resident),
            pl.BlockSpec((1, d), const),
        ],
        out_specs=pl.BlockSpec((tm, d), lambda i: (i, 0)),
        out_shape=jax.ShapeDtypeStruct((out_rows, d), F32),
        input_output_aliases={} if final else {0: 0},
        compiler_params=_cparams(("parallel",)),
    )(h, mod, norm_w, w_in, w_out, final_w)


def _pad_cols(w, width):
    return jnp.pad(w, ((0, 0), (0, 0), (0, width - w.shape[-1])))


def _prep_w_in(w_in):
    a0 = 0
    b0 = MLA_Q_RANK + MLA_KV_RANK + MLA_ROPE
    c0 = b0 + 2 * GLA_QK_W + 2 * GLA_V_W + 2 * GLA_GATE_RANK
    g0 = c0 + 3 * NA_W
    sl = lambda lo, n: w_in[:, :, lo:lo + n]
    aq = sl(a0, MLA_Q_RANK)
    akv = sl(a0 + MLA_Q_RANK, MLA_KV_RANK)
    kr0 = a0 + MLA_Q_RANK + MLA_KV_RANK
    half = MLA_ROPE // 2
    kr = sl(kr0, MLA_ROPE)
    kr_sw = jnp.concatenate([sl(kr0 + half, half), sl(kr0, half)], axis=-1)
    zeros = lambda n: jnp.zeros(w_in.shape[:2] + (n,), w_in.dtype)
    akr = jnp.concatenate([zeros(MLA_NOPE), kr, zeros(LANE - MLA_NOPE - MLA_ROPE),
                           zeros(MLA_NOPE), kr_sw, zeros(LANE - MLA_NOPE - MLA_ROPE)], axis=-1)
    bq = sl(b0, GLA_QK_W)
    bk = sl(b0 + GLA_QK_W, GLA_QK_W)
    bv = sl(b0 + 2 * GLA_QK_W, GLA_V_W)
    bg = sl(b0 + 2 * GLA_QK_W + GLA_V_W, GLA_V_W)
    lr = _pad_cols(sl(b0 + 2 * GLA_QK_W + 2 * GLA_V_W, 2 * GLA_GATE_RANK), LANE)
    cq = sl(c0, NA_W) * (NA_SCALE * LOG2E)
    ck = sl(c0 + NA_W, NA_W)
    cv = sl(c0 + 2 * NA_W, NA_W)
    gates = sl(g0, 3 * D_MODEL)
    out = jnp.concatenate([aq, lr, cq, ck, cv, bv, bg, bq, bk, akv, akr, gates], axis=-1)
    assert out.shape[-1] == Z_WIDTH
    return out.astype(BF16)


def _prep_mla_weights(w_q_up, w_kv_up):
    depth = w_q_up.shape[0]
    hd = MLA_NOPE + MLA_ROPE
    half = MLA_ROPE // 2
    wq = w_q_up.reshape(depth, MLA_Q_RANK, MLA_HEADS, hd)
    pad = jnp.zeros((depth, MLA_Q_RANK, MLA_HEADS, LANE - hd), w_q_up.dtype)
    plain = jnp.concatenate([wq, pad], axis=-1)
    swapped = jnp.concatenate([jnp.zeros_like(wq[..., :MLA_NOPE]), wq[..., MLA_NOPE + half:],
                               wq[..., MLA_NOPE:MLA_NOPE + half], pad], axis=-1)
    wq_p = jnp.concatenate([plain.reshape(depth, MLA_Q_RANK, -1), swapped.reshape(depth, MLA_Q_RANK, -1)], axis=-1)
    wkv = w_kv_up.reshape(depth, MLA_KV_RANK, MLA_HEADS, MLA_NOPE + MLA_V)
    wkn = jnp.concatenate([wkv[..., :MLA_NOPE],
                           jnp.zeros((depth, MLA_KV_RANK, MLA_HEADS, LANE - MLA_NOPE), w_kv_up.dtype)], axis=-1)
    wv = wkv[..., MLA_NOPE:]
    return (wq_p.astype(BF16), wkn.reshape(depth, MLA_KV_RANK, -1).astype(BF16),
            wv.reshape(depth, MLA_KV_RANK, -1).astype(BF16))


def _rope_tables():
    t = np.arange(SEQ)
    rows = (t // GRID_W).astype(np.float32)
    cols = (t % GRID_W).astype(np.float32)
    n_freq = MLA_ROPE // 4
    inv_freq = jnp.asarray(ROPE_BASE, F32) ** (-jnp.arange(n_freq, dtype=F32) / n_freq)
    ang = jnp.concatenate([jnp.asarray(rows)[:, None] * inv_freq, jnp.asarray(cols)[:, None] * inv_freq], axis=-1)
    cos, sin = jnp.cos(ang), jnp.sin(ang)
    cos = jnp.concatenate([cos, jnp.ones((TM_IN, MLA_ROPE // 2), F32)], axis=0)
    sin = jnp.concatenate([sin, jnp.zeros((TM_IN, MLA_ROPE // 2), F32)], axis=0)
    n = cos.shape[0]
    tail = jnp.zeros((n, LANE - MLA_NOPE - MLA_ROPE), F32)
    c_rot = jnp.concatenate([cos, cos], axis=-1)
    s_rot = jnp.concatenate([-sin, sin], axis=-1)
    nope1 = jnp.ones((n, MLA_NOPE), F32)
    nope0 = jnp.zeros((n, MLA_NOPE), F32)
    tqc = jnp.concatenate([nope1, c_rot, tail], axis=-1) * (MLA_SCALE * LOG2E)
    tqs = jnp.concatenate([nope0, s_rot, tail], axis=-1) * (MLA_SCALE * LOG2E)
    tkc = jnp.concatenate([nope0, c_rot, tail], axis=-1)
    tks = jnp.concatenate([nope0, s_rot, tail], axis=-1)
    return tqc, tqs, tkc, tks


def _na_bias_tables(rpb):
    depth = rpb.shape[0]
    n_off = 2 * NA_KH - 1
    w2 = 2 * GRID_W
    lo = GRID_W - NA_KW
    u = jnp.pad(rpb.astype(F32), ((0, 0), (0, 0), (0, 0), (lo, w2 - lo - (2 * NA_KW - 1))))
    skew = jnp.tile(u, (1, 1, 1, GRID_W))[..., :GRID_W * (w2 - 1)].reshape(depth, NA_HEADS, n_off, GRID_W, w2 - 1)
    toep = skew[..., GRID_W - 1:]
    col = np.arange(GRID_W)
    cs = np.clip(col - NA_KW // 2, 0, GRID_W - NA_KW)
    col_ok = (col[None, :] >= cs[:, None]) & (col[None, :] < cs[:, None] + NA_KW)
    toep = jnp.swapaxes(jnp.where(jnp.asarray(col_ok), toep * LOG2E, -1e30), -1, -2)
    masked = jnp.full((depth, NA_HEADS, GRID_W, GRID_W), -1e30, F32)
    n_blocks = GRID_ROWS // NA_QROWS
    tables = []
    for g in (0, 1, n_blocks - 1):
        band0 = int(np.clip(NA_QROWS * g - NA_KH // 2, 0, GRID_ROWS - NA_BAND_ROWS))
        k_rows = []
        for i in range(NA_BAND_ROWS):
            kr = band0 + i
            blocks = []
            for j in range(NA_QROWS):
                qr = NA_QROWS * g + j
                r0 = int(np.clip(qr - NA_KH // 2, 0, GRID_ROWS - NA_KH))
                blocks.append(toep[:, :, kr - qr + NA_KH - 1] if r0 <= kr < r0 + NA_KH else masked)
            k_rows.append(jnp.concatenate(blocks, axis=-1))
        tables.append(jnp.concatenate(k_rows, axis=-2))
    return jnp.stack(tables, axis=1)


def _prep_gla_gate(w_f, b_f, w_b, b_b):
    depth = w_f.shape[0]
    wg = jnp.zeros((depth, LANE, 2 * GLA_QK_W), F32)
    wg = wg.at[:, :GLA_GATE_RANK, :GLA_QK_W].set(w_f)
    wg = wg.at[:, GLA_GATE_RANK:2 * GLA_GATE_RANK, GLA_QK_W:].set(w_b)
    bg = jnp.concatenate([b_f, b_b], axis=-1)[:, None, :]
    return wg.astype(BF16), bg


def kernel(x, c, ctx, c_ctx, w_mod, b_mod, norm1_w, w_in, mla_q_norm_w, mla_kv_norm_w, mla_w_q_up, mla_w_kv_up, gla_w_gate_f, gla_b_gate_f, gla_w_gate_b, gla_b_gate_b, gla_norm_w, na_rpb, w_a_o, w_b_o, w_c_o, w_out, norm2_w, w_ffn_in, w_ffn_out, final_norm_w):
    batch, seq, d = x.shape
    depth = w_mod.shape[0]
    assert seq == SEQ and d == D_MODEL and ctx.shape[1] == CTX_LEN
    assert (batch * CTX_LEN) % TM_IN == 0
    n_lat_in = batch * SEQ // TM_IN

    mod_rows = -(-(batch + 1) // 8) * 8
    c_all = jnp.concatenate([c, c_ctx[None, :], jnp.zeros((mod_rows - batch - 1, d), F32)], axis=0)
    mod_all = _modulation(c_all, w_mod, b_mod).reshape(depth, mod_rows, 6, d)

    w_in_p = _prep_w_in(w_in)
    wq_p, wkn_p, wv_p = _prep_mla_weights(mla_w_q_up, mla_w_kv_up)
    tabs = _rope_tables()
    na_bias = _na_bias_tables(na_rpb)
    wg_p, bg_p = _prep_gla_gate(gla_w_gate_f, gla_b_gate_f, gla_w_gate_b, gla_b_gate_b)
    wa, wb, wc, wo = (w.astype(BF16) for w in (w_a_o, w_b_o, w_c_o, w_out))
    wfi, wfo = w_ffn_in.astype(BF16), w_ffn_out.astype(BF16)
    row = lambda w, i: w[i][None, :]

    h = jnp.concatenate([x.reshape(batch * SEQ, d), ctx.reshape(batch * CTX_LEN, d)], axis=0)
    for i in range(depth):
        need_ctx = i < depth - 1
        mod = mod_all[i]
        z = _inproj(h, mod, row(norm1_w, i), w_in_p[i], batch)
        q, k, v = _mla_prep(z, row(mla_q_norm_w, i), row(mla_kv_norm_w, i), wq_p[i], wkn_p[i], wv_p[i], tabs, n_lat_in)
        ya = _mla_attn(q, k, v, batch, need_ctx)
        yb_lat, yb_ctx = _gla(z, wg_p[i], bg_p[i], row(gla_norm_w, i), batch)
        yc = _na(z, na_bias[i], batch, need_ctx)
        h = _merge(h, ya, yb_lat, yb_ctx, yc, z, mod, wa[i], wb[i], wc[i], wo[i], batch, need_ctx)
        h = _ffn(h, mod, row(norm2_w, i), wfi[i], wfo[i], final_norm_w[None, :], batch, need_ctx, not need_ctx)
    return h.reshape(batch, SEQ, d)
```

```python
import functools

import jax
import jax.numpy as jnp
import numpy as np
from jax import lax
from jax.experimental import pallas as pl
from jax.experimental.pallas import tpu as pltpu

F32 = jnp.float32
BF16 = jnp.bfloat16

D_MODEL = 1024
SEQ = 2048
CTX_LEN = 256
GRID_W = 64
GRID_ROWS = SEQ // GRID_W
EPS = 1e-6
ROPE_BASE = 10000.0

MLA_HEADS = 8
MLA_Q_RANK = 384
MLA_KV_RANK = 256
MLA_NOPE = 64
MLA_ROPE = 32
MLA_V = 64
MLA_SCALE = (MLA_NOPE + MLA_ROPE) ** -0.5

GLA_HEADS = 4
GLA_DK = 64
GLA_DV = 128
GLA_GATE_RANK = 16
GLA_GATE_TAU = 16.0
GLA_CHUNK = 64
GLA_TILE = 256
GLA_TILES_PER_STEP = 2
GLA_PAIR = 2
GLA_QK_W = GLA_HEADS * GLA_DK
GLA_V_W = GLA_HEADS * GLA_DV

NA_HEADS = 8
NA_HEAD_DIM = 64
NA_KH = 8
NA_KW = 16
NA_W = NA_HEADS * NA_HEAD_DIM
NA_SCALE = NA_HEAD_DIM ** -0.5
LOG2E = 1.4426950408889634
NA_QROWS = 4
NA_BAND_ROWS = NA_KH + NA_QROWS
NA_BAND = NA_BAND_ROWS * GRID_W
NA_QBLK = NA_QROWS * GRID_W

FFN_HIDDEN = 2816
LANE = 128
BF16_SUBLANES = 16

Z_AQ = 0
Z_LR = 384
Z_CQ = 512
Z_CK = 1024
Z_CV = 1536
Z_BV = 2048
Z_BG = 2560
Z_BQ = 3072
Z_BK = 3328
Z_AKV = 3584
Z_AKR = 3840
Z_GATES = 4096
Z_WIDTH = 7168

TM_IN = 1024
TN_IN = 1024
TM_TOK = 512
TQ = 256
KEY_CHUNK = 256
MLA_KEY_CHUNK = 256
SCORE_LOOKAHEAD = 2
HEAD_GROUP = 8
VMEM_LIMIT = 56 * 1024 * 1024


def _cparams(sem):
    return pltpu.CompilerParams(dimension_semantics=sem, vmem_limit_bytes=VMEM_LIMIT)


def _dot(a, b):
    return jnp.dot(a, b, preferred_element_type=F32)


def _dot_nt(a, b):
    return lax.dot_general(a, b, (((1,), (1,)), ((), ())), preferred_element_type=F32)


def _dot_tn(a, b):
    return lax.dot_general(a, b, (((0,), (0,)), ((), ())), preferred_element_type=F32)


def _sigmoid(x):
    return 1.0 / (1.0 + jnp.exp(-x))


def _rms(x, w):
    return x * lax.rsqrt(jnp.mean(x * x, axis=-1, keepdims=True) + EPS) * w


def _split3(x):
    hi = x.astype(BF16)
    r1 = x - hi.astype(F32)
    mid = r1.astype(BF16)
    lo = (r1 - mid.astype(F32)).astype(BF16)
    return hi, mid, lo


def _mod_kernel(c_ref, w_ref, b_ref, o_ref):
    c = c_ref[...]
    a = c * _sigmoid(c)
    a_hi = a.astype(BF16)
    a_lo = (a - a_hi.astype(F32)).astype(BF16)
    w = w_ref[0]
    w_hi = w.astype(BF16)
    w_lo = (w - w_hi.astype(F32)).astype(BF16)
    o_ref[0] = _dot(a_hi, w_hi) + _dot(a_lo, w_hi) + _dot(a_hi, w_lo) + b_ref[0]


def _modulation(c_all, w_mod, b_mod):
    depth, d, n = w_mod.shape
    rows = c_all.shape[0]
    tn = 1536
    return pl.pallas_call(
        _mod_kernel,
        name="adaln_mod",
        grid=(depth, n // tn),
        in_specs=[
            pl.BlockSpec((rows, d), lambda l, j: (0, 0)),
            pl.BlockSpec((1, d, tn), lambda l, j: (l, 0, j)),
            pl.BlockSpec((1, 1, tn), lambda l, j: (l, 0, j)),
        ],
        out_specs=pl.BlockSpec((1, rows, tn), lambda l, j: (l, 0, j)),
        out_shape=jax.ShapeDtypeStruct((depth, rows, n), F32),
        compiler_params=_cparams(("parallel", "parallel")),
    )(c_all, w_mod, b_mod.reshape(depth, 1, n))


def _h_specs(tm, d, n_lat_tiles, ctx_block0):
    return (pl.BlockSpec((tm, d), lambda i: (jnp.minimum(i, n_lat_tiles - 1), 0)),
            pl.BlockSpec((tm, d), lambda i: (ctx_block0 + jnp.maximum(i - n_lat_tiles, 0), 0)))


def _inproj_kernel(hl_ref, hc_ref, mod_ref, nw_ref, w_ref, o_ref, *, n_lat_tiles):
    h = jnp.where(pl.program_id(0) < n_lat_tiles, hl_ref[...], hc_ref[...])
    y = _rms(h, nw_ref[...])
    xn = (y * (1.0 + mod_ref[0, 1:2, :]) + mod_ref[0, 0:1, :]).astype(BF16)
    for c in range(0, Z_WIDTH, TN_IN):
        o_ref[:, c:c + TN_IN] = _dot(xn, w_ref[:, c:c + TN_IN]).astype(BF16)


def _layer_spec(arr, layer, **kw):
    tail = arr.shape[1:]
    return pl.BlockSpec((None,) + tail, lambda *_: (layer,) + (0,) * len(tail), **kw)


def _mod_spec(mod_all, layer, row_of_tile):
    return pl.BlockSpec((None, 1) + mod_all.shape[2:], lambda i: (layer, row_of_tile(i), 0, 0))


def _inproj(h_lat, h_ctx, ctx_block0, mod_all, norm_w, w_in_p, layer, batch):
    d = h_lat.shape[1]
    ntok = batch * (SEQ + CTX_LEN)
    tm = TM_TOK
    n_lat_tiles = batch * SEQ // tm
    per_batch = SEQ // tm
    return pl.pallas_call(
        functools.partial(_inproj_kernel, n_lat_tiles=n_lat_tiles),
        name="in_proj",
        grid=(ntok // tm,),
        in_specs=[
            *_h_specs(tm, d, n_lat_tiles, ctx_block0),
            _mod_spec(mod_all, layer, lambda i: jnp.where(i < n_lat_tiles, i // per_batch, batch)),
            _layer_spec(norm_w, layer),
            _layer_spec(w_in_p, layer, pipeline_mode=pl.Buffered(1)),
        ],
        out_specs=pl.BlockSpec((tm, Z_WIDTH), lambda i: (i, 0)),
        out_shape=jax.ShapeDtypeStruct((ntok, Z_WIDTH), BF16),
        compiler_params=_cparams(("parallel",)),
    )(h_lat, h_ctx, mod_all, norm_w, w_in_p)


def _mla_prep_kernel(aq_ref, akv_ref, akr_ref, qnw_ref, kvnw_ref, wq_ref, wkn_ref, wv_ref,
                     tqc_ref, tqs_ref, tkc_ref, tks_ref, q_out, k_out, v_out):
    qn = _rms(aq_ref[...].astype(F32), qnw_ref[...]).astype(BF16)
    qq = _dot(qn, wq_ref[...])
    hw = MLA_HEADS * LANE
    tqc = tqc_ref[...]
    tqs = tqs_ref[...]
    for h in range(MLA_HEADS):
        lo = h * LANE
        q_out[:, lo:lo + LANE] = (qq[:, lo:lo + LANE] * tqc + qq[:, hw + lo:hw + lo + LANE] * tqs).astype(BF16)
    kvn = _rms(akv_ref[...].astype(F32), kvnw_ref[...]).astype(BF16)
    kn = _dot(kvn, wkn_ref[...])
    r = akr_ref[...].astype(F32)
    kr = r[:, :LANE] * tkc_ref[...] + r[:, LANE:] * tks_ref[...]
    for h in range(MLA_HEADS):
        lo = h * LANE
        k_out[:, lo:lo + LANE] = (kn[:, lo:lo + LANE] + kr).astype(BF16)
    v_out[...] = _dot(kvn, wv_ref[...]).astype(BF16)


def _mla_prep(z, qnw, kvnw, wq, wkn, wv, tabs, layer, n_lat_tiles):
    ntok = z.shape[0]
    tm = TM_IN
    per_batch = SEQ // tm
    tab_idx = lambda i: (jnp.where(i < n_lat_tiles, i % per_batch, per_batch), 0)
    hw = MLA_HEADS * LANE
    tab_spec = pl.BlockSpec((tm, LANE), tab_idx)
    return pl.pallas_call(
        _mla_prep_kernel,
        name="mla_prep",
        grid=(ntok // tm,),
        in_specs=[
            pl.BlockSpec((tm, MLA_Q_RANK), lambda i: (i, Z_AQ // MLA_Q_RANK)),
            pl.BlockSpec((tm, MLA_KV_RANK), lambda i: (i, Z_AKV // MLA_KV_RANK)),
            pl.BlockSpec((tm, 2 * LANE), lambda i: (i, Z_AKR // (2 * LANE))),
            _layer_spec(qnw, layer), _layer_spec(kvnw, layer),
            _layer_spec(wq, layer), _layer_spec(wkn, layer), _layer_spec(wv, layer),
            tab_spec, tab_spec, tab_spec, tab_spec,
        ],
        out_specs=[
            pl.BlockSpec((tm, hw), lambda i: (i, 0)),
            pl.BlockSpec((tm, hw), lambda i: (i, 0)),
            pl.BlockSpec((tm, MLA_HEADS * MLA_V), lambda i: (i, 0)),
        ],
        out_shape=[
            jax.ShapeDtypeStruct((ntok, hw), BF16),
            jax.ShapeDtypeStruct((ntok, hw), BF16),
            jax.ShapeDtypeStruct((ntok, MLA_HEADS * MLA_V), BF16),
        ],
        compiler_params=_cparams(("parallel",)),
    )(z, z, z, qnw, kvnw, wq, wkn, wv, *tabs)


def _attend_t(qs, n_chunks, chunk):
    if len(qs) > HEAD_GROUP:
        outs = []
        for g in range(0, len(qs), HEAD_GROUP):
            outs += _attend_t(qs[g:g + HEAD_GROUP], n_chunks, lambda h, c, g=g: chunk(g + h, c))
        return outs
    heads = range(len(qs))
    dv = chunk(0, 0)[1].shape[0]
    m = o = None

    def scores(c):
        parts = [chunk(h, c) for h in heads]
        s = [_dot_nt(parts[h][0], qs[h]) for h in heads]
        return parts, [s[h] if parts[h][2] is None else s[h] + parts[h][2] for h in heads]

    ahead = [scores(c) for c in range(min(SCORE_LOOKAHEAD, n_chunks))]
    for c in range(n_chunks):
        parts, s = ahead.pop(0)
        if c + SCORE_LOOKAHEAD < n_chunks:
            ahead.append(scores(c + SCORE_LOOKAHEAD))
        mc =[jnp.max(s[h], axis=0, keepdims=True) for h in heads]
        ones = jnp.ones((BF16_SUBLANES, parts[0][1].shape[1]), BF16)
        vt1 = [jnp.concatenate([parts[h][1], ones], axis=0) for h in heads]
        if m is None:
            m = mc
            o = [_dot(vt1[h], jnp.exp2(s[h] - m[h]).astype(BF16)) for h in heads]
        else:
            m_new = [jnp.maximum(m[h], mc[h]) for h in heads]
            alpha = [jnp.exp2(m[h] - m_new[h]) for h in heads]
            o = [o[h] * alpha[h] + _dot(vt1[h], jnp.exp2(s[h] - m_new[h]).astype(BF16)) for h in heads]
            m = m_new
    return [o[h][:dv] / o[h][dv:dv + 1] for h in heads]


def _transpose_rows(src_ref, dst_ref, rows):
    for r in range(0, rows, TQ):
        dst_ref[:, r:r + TQ] = src_ref[r:r + TQ, :].astype(F32).T.astype(BF16)


def _mla_attn_kernel(q_ref, kl_ref, kc_ref, vl_ref, vc_ref, o_ref, vlt_s, vct_s, *, n_lat_q):
    t = pl.program_id(1)

    @pl.when(t == 0)
    def _():
        _transpose_rows(vl_ref, vlt_s, SEQ)
        _transpose_rows(vc_ref, vct_s, CTX_LEN)

    def attend(with_lat):
        def chunk(h, c):
            cols = slice(h * LANE, (h + 1) * LANE)
            rows = slice(h * MLA_V, (h + 1) * MLA_V)
            if c == 0:
                return kc_ref[:, cols], vct_s[rows, :], None
            keys = slice((c - 1) * MLA_KEY_CHUNK, c * MLA_KEY_CHUNK)
            return kl_ref[keys, cols], vlt_s[rows, keys], None

        qs = [q_ref[:, h * LANE:(h + 1) * LANE] for h in range(MLA_HEADS)]
        outs = _attend_t(qs, 1 + (SEQ // MLA_KEY_CHUNK if with_lat else 0), chunk)
        o_ref[...] = jnp.concatenate(outs, axis=0).T.astype(BF16)

    pl.when(t < n_lat_q)(lambda: attend(True))
    pl.when(t >= n_lat_q)(lambda: attend(False))


def _q_row_block(n_lat_q, ctx_base):
    return lambda b, t: jnp.where(t < n_lat_q, b * n_lat_q + t, ctx_base + b)


def _mla_attn(q, k, v, batch, need_ctx):
    ntok = q.shape[0]
    n_lat_q = SEQ // TQ
    ctx_base = batch * SEQ // CTX_LEN
    qrow = _q_row_block(n_lat_q, ctx_base)
    hw = MLA_HEADS * LANE
    vw = MLA_HEADS * MLA_V
    return pl.pallas_call(
        functools.partial(_mla_attn_kernel, n_lat_q=n_lat_q),
        name="mla_attn",
        grid=(batch, n_lat_q + (1 if need_ctx else 0)),
        in_specs=[
            pl.BlockSpec((TQ, hw), lambda b, t: (qrow(b, t), 0)),
            pl.BlockSpec((SEQ, hw), lambda b, t: (b, 0)),
            pl.BlockSpec((CTX_LEN, hw), lambda b, t: (ctx_base + b, 0)),
            pl.BlockSpec((SEQ, vw), lambda b, t: (b, 0)),
            pl.BlockSpec((CTX_LEN, vw), lambda b, t: (ctx_base + b, 0)),
        ],
        out_specs=pl.BlockSpec((TQ, vw), lambda b, t: (qrow(b, t), 0)),
        out_shape=jax.ShapeDtypeStruct((ntok, vw), BF16),
        scratch_shapes=[pltpu.VMEM((vw, SEQ), BF16), pltpu.VMEM((vw, CTX_LEN), BF16)],
        compiler_params=_cparams(("parallel", "arbitrary")),
    )(q, k, k, v, v)


def _gla_kernel(ql_ref, qc_ref, kl_ref, kc_ref, vl_ref, vc_ref, gl_ref, gc_ref, lrl_ref, lrc_ref,
                wg_ref, bg_ref, nw_ref, yl_ref, yc_ref,
                v_s, qd_s, ke_s, dec_s, o_s, st_s):
    n_tok = SEQ + CTX_LEN
    n_chunk = n_tok // GLA_CHUNK
    ctx_chunks = CTX_LEN // GLA_CHUNK
    ck = GLA_CHUNK
    tile = GLA_TILE
    per_tile = tile // ck
    qk_w = GLA_QK_W
    pair_k = GLA_PAIR * GLA_DK
    pair_v = GLA_PAIR * GLA_DV
    n_pairs = GLA_HEADS // GLA_PAIR

    v_s[0:SEQ, :] = vl_ref[...]
    v_s[SEQ:n_tok, :] = vc_ref[...]
    st_s[...] = jnp.zeros_like(st_s)

    row = lax.broadcasted_iota(jnp.int32, (tile, tile), 0)
    col = lax.broadcasted_iota(jnp.int32, (tile, tile), 1)
    same_chunk = (row // ck) == (col // ck)
    keep = (same_chunk & (col <= row), same_chunk & (col >= row))
    tri3 = tuple(jnp.concatenate([jnp.where(kp, 1.0, 0.0).astype(BF16)] * 3, axis=1) for kp in keep)
    q_head = lax.broadcasted_iota(jnp.int32, (tile, qk_w), 1) // GLA_DK
    bd_pair = (lax.broadcasted_iota(jnp.int32, (pair_k, pair_v), 0) // GLA_DK
               == lax.broadcasted_iota(jnp.int32, (pair_k, pair_v), 1) // GLA_DV)

    def log_decay(lr):
        zz = _dot(lr, wg_ref[...]) + bg_ref[...]
        return (jnp.minimum(zz, 0.0) - jnp.log1p(jnp.exp(-jnp.abs(zz)))) * (1.0 / GLA_GATE_TAU)

    def prologue(tiles):
        nt = range(len(tiles))
        dirs = range(2)
        heads = range(GLA_HEADS)
        rows = [pl.ds(tiles[t][4], tile) for t in nt]
        q = [tiles[t][0].astype(F32) * (GLA_DK ** -0.5) for t in nt]
        k = [tiles[t][1].astype(F32) for t in nt]
        la = [log_decay(tiles[t][3]) for t in nt]
        cum = [[_dot(tri3[d], jnp.concatenate(_split3(la[t][:, d * qk_w:(d + 1) * qk_w]), axis=0))
                for d in dirs] for t in nt]
        edge = (ck - 1, 0)
        lasts = [[[cum[t][d][c * ck + edge[d]:c * ck + edge[d] + 1, :] for c in range(per_tile)]
                  for d in dirs] for t in nt]
        last = [[jnp.concatenate([jnp.broadcast_to(x, (ck, qk_w)) for x in lasts[t][d]], axis=0)
                 for d in dirs] for t in nt]
        q_dec = [[(q[t] * jnp.exp(cum[t][d])).astype(BF16) for d in dirs] for t in nt]
        k_inv = [[(k[t] * jnp.exp(-cum[t][d])).astype(BF16) for d in dirs] for t in nt]
        for t in nt:
            for d in dirs:
                qd_s[d, rows[t], :] = q_dec[t][d]
                ke_s[d, rows[t], :] = (k[t] * jnp.exp(last[t][d] - cum[t][d])).astype(BF16)
                for c in range(per_tile):
                    dec_s[d, tiles[t][4] // ck + c] = jnp.broadcast_to(jnp.exp(lasts[t][d][c]), (8, qk_w))
        qm = [[[jnp.where(q_head == h, q_dec[t][d], jnp.zeros((), BF16)) for h in heads] for d in dirs] for t in nt]
        s = [[[_dot_nt(qm[t][d][h], k_inv[t][d]) for h in heads] for d in dirs] for t in nt]
        att = [[[jnp.where(keep[d], s[t][d][h], 0.0).astype(BF16) for h in heads] for d in dirs] for t in nt]
        o = [[[_dot(att[t][d][h], tiles[t][2][:, h * GLA_DV:(h + 1) * GLA_DV]) for h in heads]
              for d in dirs] for t in nt]
        for t in nt:
            for d in dirs:
                for h in heads:
                    o_s[d, rows[t], h * GLA_DV:(h + 1) * GLA_DV] = o[t][d][h]

    def lat_tiles(i, carry):
        tiles = []
        for j in range(GLA_TILES_PER_STEP):
            r0 = pl.multiple_of((i * GLA_TILES_PER_STEP + j) * tile, tile)
            rows = pl.ds(r0, tile)
            tiles.append((ql_ref[rows, :], kl_ref[rows, :], vl_ref[rows, :], lrl_ref[rows, :], r0))
        prologue(tiles)
        return carry

    lax.fori_loop(0, SEQ // (tile * GLA_TILES_PER_STEP), lat_tiles, 0)
    for r0 in range(0, CTX_LEN, tile):
        rows = slice(r0, r0 + tile)
        prologue([(qc_ref[rows, :], kc_ref[rows, :], vc_ref[rows, :], lrc_ref[rows, :], SEQ + r0)])

    def scan_step(n, d):
        rows = pl.ds(pl.multiple_of(n * ck, ck), ck)
        dec = dec_s[d, n]
        for p in range(n_pairs):
            kcols = slice(p * pair_k, (p + 1) * pair_k)
            vcols = slice(p * pair_v, (p + 1) * pair_v)
            st = st_s[d, p]
            o_s[d, rows, vcols] = o_s[d, rows, vcols] + _dot(qd_s[d, rows, kcols], st.astype(BF16))
            kv = _dot_tn(ke_s[d, rows, kcols], v_s[rows, vcols])
            decay_rows = jnp.broadcast_to(dec[0:1, kcols], (pair_v, pair_k)).T
            st_s[d, p] = st * decay_rows + jnp.where(bd_pair, kv, 0.0)

    def body(i, carry):
        nf = jnp.where(i < ctx_chunks, n_chunk - ctx_chunks + i, i - ctx_chunks)
        nb = n_chunk - 1 - i
        scan_step(nf, 0)
        scan_step(nb, 1)
        return carry

    lax.fori_loop(0, n_chunk, body, 0, unroll=2)

    nw = nw_ref[...]

    def finish(r0, rows, g):
        o = o_s[0, r0:r0 + rows, :] + o_s[1, r0:r0 + rows, :]
        g = g.astype(F32)
        parts = []
        for h in range(GLA_HEADS):
            parts.append(_rms(o[:, h * GLA_DV:(h + 1) * GLA_DV], nw))
        y = jnp.concatenate(parts, axis=1)
        return (y * (g * _sigmoid(g))).astype(BF16)

    rt = GLA_TILE
    for i in range(SEQ // rt):
        yl_ref[i * rt:(i + 1) * rt, :] = finish(i * rt, rt, gl_ref[i * rt:(i + 1) * rt, :])
    yc_ref[...] = finish(SEQ, CTX_LEN, gc_ref[...])


def _gla(z, wg, bg, nw, layer, batch):
    ctx_base = batch * SEQ // CTX_LEN
    n_tok = SEQ + CTX_LEN

    def lat(width, col0):
        return pl.BlockSpec((SEQ, width), lambda b: (b, col0 // width))

    def ctx(width, col0):
        return pl.BlockSpec((CTX_LEN, width), lambda b: (ctx_base + b, col0 // width))

    return pl.pallas_call(
        _gla_kernel,
        name="gla_scan",
        grid=(batch,),
        in_specs=[
            lat(GLA_QK_W, Z_BQ), ctx(GLA_QK_W, Z_BQ),
            lat(GLA_QK_W, Z_BK), ctx(GLA_QK_W, Z_BK),
            lat(GLA_V_W, Z_BV), ctx(GLA_V_W, Z_BV),
            lat(GLA_V_W, Z_BG), ctx(GLA_V_W, Z_BG),
            lat(LANE, Z_LR), ctx(LANE, Z_LR),
            _layer_spec(wg, layer), _layer_spec(bg, layer), _layer_spec(nw, layer),
        ],
        out_specs=[
            pl.BlockSpec((SEQ, GLA_V_W), lambda b: (b, 0)),
            pl.BlockSpec((CTX_LEN, GLA_V_W), lambda b: (b, 0)),
        ],
        out_shape=[
            jax.ShapeDtypeStruct((batch * SEQ, GLA_V_W), BF16),
            jax.ShapeDtypeStruct((batch * CTX_LEN, GLA_V_W), BF16),
        ],
        scratch_shapes=[
            pltpu.VMEM((n_tok, GLA_V_W), BF16),
            pltpu.VMEM((2, n_tok, GLA_QK_W), BF16),
            pltpu.VMEM((2, n_tok, GLA_QK_W), BF16),
            pltpu.VMEM((2, n_tok // GLA_CHUNK, 8, GLA_QK_W), F32),
            pltpu.VMEM((2, n_tok, GLA_V_W), F32),
            pltpu.VMEM((2, GLA_HEADS // GLA_PAIR, GLA_PAIR * GLA_DK, GLA_PAIR * GLA_DV), F32),
        ],
        compiler_params=_cparams(("parallel",)),
    )(z, z, z, z, z, z, z, z, z, z, wg, bg, nw)


def _na_kernel(q_ref, kl_ref, kc_ref, vl_ref, vc_ref, bias_ref, o_ref, vlt_s, vct_s, *, n_lat_q):
    lane = lax.broadcasted_iota(jnp.int32, (TQ, LANE), 1)
    t = pl.program_id(1)

    @pl.when(t == 0)
    def _():
        _transpose_rows(vl_ref, vlt_s, SEQ)
        _transpose_rows(vc_ref, vct_s, CTX_LEN)

    def attend(with_band):
        if with_band:
            r0 = jnp.clip(NA_QROWS * t - NA_KH // 2, 0, GRID_ROWS - NA_BAND_ROWS)
            start = pl.multiple_of(r0 * GRID_W, NA_QBLK)
            kind = jnp.where(t == 0, 0, jnp.where(t == n_lat_q - 1, 2, 1))

        def chunk(h, c):
            cols = slice((h // 2) * LANE, (h // 2 + 1) * LANE)
            rows = slice(h * NA_HEAD_DIM, (h + 1) * NA_HEAD_DIM)
            if c == 0:
                return kc_ref[:, cols], vct_s[rows, :], None
            off = (c - 1) * KEY_CHUNK
            keys = pl.ds(pl.multiple_of(start + off, KEY_CHUNK), KEY_CHUNK)
            return kl_ref[keys, cols], vlt_s[rows, keys], bias_ref[kind, h, off:off + KEY_CHUNK, :]

        qs = []
        for h in range(NA_HEADS):
            head_lanes = lane < NA_HEAD_DIM if h % 2 == 0 else lane >= NA_HEAD_DIM
            qs.append(jnp.where(head_lanes, q_ref[:, (h // 2) * LANE:(h // 2 + 1) * LANE], jnp.zeros((), BF16)))
        outs = _attend_t(qs, 1 + (NA_BAND // KEY_CHUNK if with_band else 0), chunk)
        o_ref[...] = jnp.concatenate(outs, axis=0).T.astype(BF16)

    pl.when(t < n_lat_q)(lambda: attend(True))
    pl.when(t >= n_lat_q)(lambda: attend(False))


def _na(z, bias, layer, batch, need_ctx):
    ntok = z.shape[0]
    n_lat_q = SEQ // TQ
    ctx_base = batch * SEQ // CTX_LEN
    qrow = _q_row_block(n_lat_q, ctx_base)
    w = NA_W
    return pl.pallas_call(
        functools.partial(_na_kernel, n_lat_q=n_lat_q),
        name="na_attn",
        grid=(batch, n_lat_q + (1 if need_ctx else 0)),
        in_specs=[
            pl.BlockSpec((TQ, w), lambda b, t: (qrow(b, t), Z_CQ // w)),
            pl.BlockSpec((SEQ, w), lambda b, t: (b, Z_CK // w)),
            pl.BlockSpec((CTX_LEN, w), lambda b, t: (ctx_base + b, Z_CK // w)),
            pl.BlockSpec((SEQ, w), lambda b, t: (b, Z_CV // w)),
            pl.BlockSpec((CTX_LEN, w), lambda b, t: (ctx_base + b, Z_CV // w)),
            _layer_spec(bias, layer, pipeline_mode=pl.Buffered(1)),
        ],
        out_specs=pl.BlockSpec((TQ, w), lambda b, t: (qrow(b, t), 0)),
        out_shape=jax.ShapeDtypeStruct((ntok, w), BF16),
        scratch_shapes=[pltpu.VMEM((w, SEQ), BF16), pltpu.VMEM((w, CTX_LEN), BF16)],
        compiler_params=_cparams(("parallel", "arbitrary")),
    )(z, z, z, z, z, bias)


def _merge_kernel(hl_ref, hc_ref, ya_ref, ybl_ref, ybc_ref, yc_ref, ga_ref, gb_ref, gc_ref, mod_ref,
                  wa_ref, wb_ref, wc_ref, wo_ref, o_ref, *, n_lat_tiles):
    is_lat = pl.program_id(0) < n_lat_tiles
    h = jnp.where(is_lat, hl_ref[...], hc_ref[...])
    yb = jnp.where(is_lat, ybl_ref[...], ybc_ref[...])
    m = (_sigmoid(ga_ref[...].astype(F32)) * _dot(ya_ref[...], wa_ref[...])
         + _sigmoid(gb_ref[...].astype(F32)) * _dot(yb, wb_ref[...])
         + _sigmoid(gc_ref[...].astype(F32)) * _dot(yc_ref[...], wc_ref[...]))
    o_ref[...] = h + mod_ref[0, 2:3, :] * _dot(m.astype(BF16), wo_ref[...])


def _merge(h_lat, h_ctx, ctx_block0, ya, yb_lat, yb_ctx, yc, z, mod_all, wa, wb, wc, wo, layer, batch, need_ctx):
    d = h_lat.shape[1]
    ntok = batch * (SEQ + CTX_LEN)
    tm = TM_TOK
    n_lat_tiles = batch * SEQ // tm
    n_tiles = ntok // tm if need_ctx else n_lat_tiles
    per_batch = SEQ // tm
    bw = wa.shape[1]
    tok = lambda width: pl.BlockSpec((tm, width), lambda i: (i, 0))
    gate = lambda g: pl.BlockSpec((tm, d), lambda i: (i, Z_GATES // d + g))
    return pl.pallas_call(
        functools.partial(_merge_kernel, n_lat_tiles=n_lat_tiles),
        name="merge",
        grid=(n_tiles,),
        in_specs=[
            *_h_specs(tm, d, n_lat_tiles, ctx_block0), tok(bw),
            pl.BlockSpec((tm, bw), lambda i: (jnp.minimum(i, n_lat_tiles - 1), 0)),
            pl.BlockSpec((tm, bw), lambda i: (jnp.maximum(i - n_lat_tiles, 0), 0)),
            tok(bw), gate(0), gate(1), gate(2),
            _mod_spec(mod_all, layer, lambda i: jnp.where(i < n_lat_tiles, i // per_batch, batch)),
            _layer_spec(wa, layer), _layer_spec(wb, layer), _layer_spec(wc, layer), _layer_spec(wo, layer),
        ],
        out_specs=tok(d),
        out_shape=jax.ShapeDtypeStruct((ntok, d), F32),
        compiler_params=_cparams(("parallel",)),
    )(h_lat, h_ctx, ya, yb_lat, yb_ctx, yc, z, z, z, mod_all, wa, wb, wc, wo)


def _ffn_kernel(h_ref, mod_ref, nw_ref, win_ref, wout_ref, fw_ref, o_ref, *, final):
    h = h_ref[...]
    u = (_rms(h, nw_ref[...]) * (1.0 + mod_ref[0, 4:5, :]) + mod_ref[0, 3:4, :]).astype(BF16)
    half = FFN_HIDDEN // 2
    acc = None
    for c in range(2):
        g = _dot(u, win_ref[:, c * half:(c + 1) * half])
        up = _dot(u, win_ref[:, FFN_HIDDEN + c * half:FFN_HIDDEN + (c + 1) * half])
        act = (g * _sigmoid(g) * up).astype(BF16)
        part = _dot(act, wout_ref[c * half:(c + 1) * half, :])
        acc = part if acc is None else acc + part
    out = h + mod_ref[0, 5:6, :] * acc
    if final:
        out = _rms(out, fw_ref[...])
    o_ref[...] = out


def _ffn(h, mod_all, norm_w, w_in, w_out, final_w, layer, batch, need_ctx, final):
    ntok, d = h.shape
    tm = TM_TOK
    n_lat_tiles = batch * SEQ // tm
    n_tiles = ntok // tm if need_ctx else n_lat_tiles
    per_batch = SEQ // tm
    resident = dict(pipeline_mode=pl.Buffered(1))
    out_rows = n_tiles * tm if final else ntok
    return pl.pallas_call(
        functools.partial(_ffn_kernel, final=final),
        name="ffn",
        grid=(n_tiles,),
        in_specs=[
            pl.BlockSpec((tm, d), lambda i: (i, 0)),
            _mod_spec(mod_all, layer, lambda i: jnp.where(i < n_lat_tiles, i // per_batch, batch)),
            _layer_spec(norm_w, layer),
            _layer_spec(w_in, layer, **resident),
            _layer_spec(w_out, layer, **resident),
            pl.BlockSpec((1, d), lambda i: (0, 0)),
        ],
        out_specs=pl.BlockSpec((tm, d), lambda i: (i, 0)),
        out_shape=jax.ShapeDtypeStruct((out_rows, d), F32),
        compiler_params=_cparams(("parallel",)),
    )(h, mod_all, norm_w, w_in, w_out, final_w)


def _pad_cols(w, width):
    return jnp.pad(w, ((0, 0), (0, 0), (0, width - w.shape[-1])))


def _prep_w_in(w_in):
    a0 = 0
    b0 = MLA_Q_RANK + MLA_KV_RANK + MLA_ROPE
    c0 = b0 + 2 * GLA_QK_W + 2 * GLA_V_W + 2 * GLA_GATE_RANK
    g0 = c0 + 3 * NA_W
    sl = lambda lo, n: w_in[:, :, lo:lo + n]
    aq = sl(a0, MLA_Q_RANK)
    akv = sl(a0 + MLA_Q_RANK, MLA_KV_RANK)
    kr0 = a0 + MLA_Q_RANK + MLA_KV_RANK
    half = MLA_ROPE // 2
    kr = sl(kr0, MLA_ROPE)
    kr_sw = jnp.concatenate([sl(kr0 + half, half), sl(kr0, half)], axis=-1)
    zeros = lambda n: jnp.zeros(w_in.shape[:2] + (n,), w_in.dtype)
    akr = jnp.concatenate([zeros(MLA_NOPE), kr, zeros(LANE - MLA_NOPE - MLA_ROPE),
                           zeros(MLA_NOPE), kr_sw, zeros(LANE - MLA_NOPE - MLA_ROPE)], axis=-1)
    bq = sl(b0, GLA_QK_W)
    bk = sl(b0 + GLA_QK_W, GLA_QK_W)
    bv = sl(b0 + 2 * GLA_QK_W, GLA_V_W)
    bg = sl(b0 + 2 * GLA_QK_W + GLA_V_W, GLA_V_W)
    lr = _pad_cols(sl(b0 + 2 * GLA_QK_W + 2 * GLA_V_W, 2 * GLA_GATE_RANK), LANE)
    cq = sl(c0, NA_W) * (NA_SCALE * LOG2E)
    ck = sl(c0 + NA_W, NA_W)
    cv = sl(c0 + 2 * NA_W, NA_W)
    gates = sl(g0, 3 * D_MODEL)
    out = jnp.concatenate([aq, lr, cq, ck, cv, bv, bg, bq, bk, akv, akr, gates], axis=-1)
    assert out.shape[-1] == Z_WIDTH
    return out.astype(BF16)


def _prep_mla_weights(w_q_up, w_kv_up):
    depth = w_q_up.shape[0]
    hd = MLA_NOPE + MLA_ROPE
    half = MLA_ROPE // 2
    wq = w_q_up.reshape(depth, MLA_Q_RANK, MLA_HEADS, hd)
    pad = jnp.zeros((depth, MLA_Q_RANK, MLA_HEADS, LANE - hd), w_q_up.dtype)
    plain = jnp.concatenate([wq, pad], axis=-1)
    swapped = jnp.concatenate([jnp.zeros_like(wq[..., :MLA_NOPE]), wq[..., MLA_NOPE + half:],
                               wq[..., MLA_NOPE:MLA_NOPE + half], pad], axis=-1)
    wq_p = jnp.concatenate([plain.reshape(depth, MLA_Q_RANK, -1), swapped.reshape(depth, MLA_Q_RANK, -1)], axis=-1)
    wkv = w_kv_up.reshape(depth, MLA_KV_RANK, MLA_HEADS, MLA_NOPE + MLA_V)
    wkn = jnp.concatenate([wkv[..., :MLA_NOPE],
                           jnp.zeros((depth, MLA_KV_RANK, MLA_HEADS, LANE - MLA_NOPE), w_kv_up.dtype)], axis=-1)
    wv = wkv[..., MLA_NOPE:]
    return (wq_p.astype(BF16), wkn.reshape(depth, MLA_KV_RANK, -1).astype(BF16),
            wv.reshape(depth, MLA_KV_RANK, -1).astype(BF16))


def _rope_tables():
    t = np.arange(SEQ)
    rows = (t // GRID_W).astype(np.float32)
    cols = (t % GRID_W).astype(np.float32)
    n_freq = MLA_ROPE // 4
    inv_freq = jnp.asarray(ROPE_BASE, F32) ** (-jnp.arange(n_freq, dtype=F32) / n_freq)
    ang = jnp.concatenate([jnp.asarray(rows)[:, None] * inv_freq, jnp.asarray(cols)[:, None] * inv_freq], axis=-1)
    cos, sin = jnp.cos(ang), jnp.sin(ang)
    cos = jnp.concatenate([cos, jnp.ones((TM_IN, MLA_ROPE // 2), F32)], axis=0)
    sin = jnp.concatenate([sin, jnp.zeros((TM_IN, MLA_ROPE // 2), F32)], axis=0)
    n = cos.shape[0]
    tail = jnp.zeros((n, LANE - MLA_NOPE - MLA_ROPE), F32)
    c_rot = jnp.concatenate([cos, cos], axis=-1)
    s_rot = jnp.concatenate([-sin, sin], axis=-1)
    nope1 = jnp.ones((n, MLA_NOPE), F32)
    nope0 = jnp.zeros((n, MLA_NOPE), F32)
    tqc = jnp.concatenate([nope1, c_rot, tail], axis=-1) * (MLA_SCALE * LOG2E)
    tqs = jnp.concatenate([nope0, s_rot, tail], axis=-1) * (MLA_SCALE * LOG2E)
    tkc = jnp.concatenate([nope0, c_rot, tail], axis=-1)
    tks = jnp.concatenate([nope0, s_rot, tail], axis=-1)
    return tqc, tqs, tkc, tks


def _na_bias_tables(rpb):
    depth = rpb.shape[0]
    n_off = 2 * NA_KH - 1
    w2 = 2 * GRID_W
    lo = GRID_W - NA_KW
    u = jnp.pad(rpb.astype(F32), ((0, 0), (0, 0), (0, 0), (lo, w2 - lo - (2 * NA_KW - 1))))
    skew = jnp.tile(u, (1, 1, 1, GRID_W))[..., :GRID_W * (w2 - 1)].reshape(depth, NA_HEADS, n_off, GRID_W, w2 - 1)
    toep = skew[..., GRID_W - 1:]
    col = np.arange(GRID_W)
    cs = np.clip(col - NA_KW // 2, 0, GRID_W - NA_KW)
    col_ok = (col[None, :] >= cs[:, None]) & (col[None, :] < cs[:, None] + NA_KW)
    toep = jnp.swapaxes(jnp.where(jnp.asarray(col_ok), toep * LOG2E, -1e30), -1, -2)
    masked = jnp.full((depth, NA_HEADS, GRID_W, GRID_W), -1e30, F32)
    n_blocks = GRID_ROWS // NA_QROWS
    tables = []
    for g in (0, 1, n_blocks - 1):
        band0 = int(np.clip(NA_QROWS * g - NA_KH // 2, 0, GRID_ROWS - NA_BAND_ROWS))
        k_rows = []
        for i in range(NA_BAND_ROWS):
            kr = band0 + i
            blocks = []
            for j in range(NA_QROWS):
                qr = NA_QROWS * g + j
                r0 = int(np.clip(qr - NA_KH // 2, 0, GRID_ROWS - NA_KH))
                blocks.append(toep[:, :, kr - qr + NA_KH - 1] if r0 <= kr < r0 + NA_KH else masked)
            k_rows.append(jnp.concatenate(blocks, axis=-1))
        tables.append(jnp.concatenate(k_rows, axis=-2))
    return jnp.stack(tables, axis=1)


def _prep_gla_gate(w_f, b_f, w_b, b_b):
    depth = w_f.shape[0]
    wg = jnp.zeros((depth, LANE, 2 * GLA_QK_W), F32)
    wg = wg.at[:, :GLA_GATE_RANK, :GLA_QK_W].set(w_f)
    wg = wg.at[:, GLA_GATE_RANK:2 * GLA_GATE_RANK, GLA_QK_W:].set(w_b)
    bg = jnp.concatenate([b_f, b_b], axis=-1)[:, None, :]
    return wg.astype(BF16), bg


def kernel(x, c, ctx, c_ctx, w_mod, b_mod, norm1_w, w_in, mla_q_norm_w, mla_kv_norm_w, mla_w_q_up, mla_w_kv_up, gla_w_gate_f, gla_b_gate_f, gla_w_gate_b, gla_b_gate_b, gla_norm_w, na_rpb, w_a_o, w_b_o, w_c_o, w_out, norm2_w, w_ffn_in, w_ffn_out, final_norm_w):
    batch, seq, d = x.shape
    depth = w_mod.shape[0]
    assert seq == SEQ and d == D_MODEL and ctx.shape[1] == CTX_LEN
    assert (batch * CTX_LEN) % TM_IN == 0
    n_lat_in = batch * SEQ // TM_IN

    mod_rows = -(-(batch + 1) // 8) * 8
    c_all = jnp.concatenate([c, c_ctx[None, :], jnp.zeros((mod_rows - batch - 1, d), F32)], axis=0)
    mod_all = _modulation(c_all, w_mod, b_mod).reshape(depth, mod_rows, 6, d)

    w_in_p = _prep_w_in(w_in)
    wq_p, wkn_p, wv_p = _prep_mla_weights(mla_w_q_up, mla_w_kv_up)
    tabs = _rope_tables()
    na_bias = _na_bias_tables(na_rpb)
    wg_p, bg_p = _prep_gla_gate(gla_w_gate_f, gla_b_gate_f, gla_w_gate_b, gla_b_gate_b)
    wa, wb, wc, wo = (w.astype(BF16) for w in (w_a_o, w_b_o, w_c_o, w_out))
    wfi, wfo = w_ffn_in.astype(BF16), w_ffn_out.astype(BF16)
    n1, n2, qnw, kvnw, gnw = (w[:, None, :] for w in (norm1_w, norm2_w, mla_q_norm_w, mla_kv_norm_w, gla_norm_w))

    hs = (x.reshape(batch * SEQ, d), ctx.reshape(batch * CTX_LEN, d), 0)
    for i in range(depth):
        need_ctx = i < depth - 1
        z = _inproj(*hs, mod_all, n1, w_in_p, i, batch)
        q, k, v = _mla_prep(z, qnw, kvnw, wq_p, wkn_p, wv_p, tabs, i, n_lat_in)
        ya = _mla_attn(q, k, v, batch, need_ctx)
        yb_lat, yb_ctx = _gla(z, wg_p, bg_p, gnw, i, batch)
        yc = _na(z, na_bias, i, batch, need_ctx)
        h = _merge(*hs, ya, yb_lat, yb_ctx, yc, z, mod_all, wa, wb, wc, wo, i, batch, need_ctx)
        h = _ffn(h, mod_all, n2, wfi, wfo, final_norm_w[None, :], i, batch, need_ctx, not need_ctx)
        hs = (h, h, batch * SEQ // TM_TOK)
    return h.reshape(batch, SEQ, d)
```

```python
import functools

import jax
import jax.numpy as jnp
import numpy as np
from jax import lax
from jax.experimental import pallas as pl
from jax.experimental.pallas import tpu as pltpu

F32 = jnp.float32
BF16 = jnp.bfloat16

D_MODEL = 1024
SEQ = 2048
CTX_LEN = 256
GRID_W = 64
GRID_ROWS = SEQ // GRID_W
EPS = 1e-6
ROPE_BASE = 10000.0

MLA_HEADS = 8
MLA_Q_RANK = 384
MLA_KV_RANK = 256
MLA_NOPE = 64
MLA_ROPE = 32
MLA_V = 64
MLA_SCALE = (MLA_NOPE + MLA_ROPE) ** -0.5

GLA_HEADS = 4
GLA_DK = 64
GLA_DV = 128
GLA_GATE_RANK = 16
GLA_GATE_TAU = 16.0
GLA_CHUNK = 64
GLA_TILE = 256
GLA_TILES_PER_STEP = 2
GLA_PAIR = 2
GLA_QK_W = GLA_HEADS * GLA_DK
GLA_V_W = GLA_HEADS * GLA_DV

NA_HEADS = 8
NA_HEAD_DIM = 64
NA_KH = 8
NA_KW = 16
NA_W = NA_HEADS * NA_HEAD_DIM
NA_SCALE = NA_HEAD_DIM ** -0.5
LOG2E = 1.4426950408889634
NA_QROWS = 4
NA_BAND_ROWS = NA_KH + NA_QROWS
NA_BAND = NA_BAND_ROWS * GRID_W
NA_QBLK = NA_QROWS * GRID_W

FFN_HIDDEN = 2816
MXU_DIM = 256
_FFN_SPLIT = (FFN_HIDDEN // MXU_DIM + 1) // 2 * MXU_DIM
FFN_CHUNKS = ((0, _FFN_SPLIT), (_FFN_SPLIT, FFN_HIDDEN))
LANE = 128
BF16_SUBLANES = 16

Z_AQ = 0
Z_LR = 384
Z_CQ = 512
Z_CK = 1024
Z_CV = 1536
Z_BV = 2048
Z_BG = 2560
Z_BQ = 3072
Z_BK = 3328
Z_AKV = 3584
Z_AKR = 3840
Z_GATES = 4096
Z_WIDTH = 7168

TM_IN = 1024
TN_IN = 1024
TM_TOK = 512
TQ = 256
KEY_CHUNK = 256
MLA_KEY_CHUNK = 256
SCORE_LOOKAHEAD = 2
HEAD_GROUP = 8
VMEM_LIMIT = 56 * 1024 * 1024


def _cparams(sem):
    return pltpu.CompilerParams(dimension_semantics=sem, vmem_limit_bytes=VMEM_LIMIT)


def _dot(a, b):
    return jnp.dot(a, b, preferred_element_type=F32)


def _dot_nt(a, b):
    return lax.dot_general(a, b, (((1,), (1,)), ((), ())), preferred_element_type=F32)


def _dot_tn(a, b):
    return lax.dot_general(a, b, (((0,), (0,)), ((), ())), preferred_element_type=F32)


def _sigmoid(x):
    return 0.5 * jnp.tanh(0.5 * x) + 0.5


def _rms(x, w):
    return x * lax.rsqrt(jnp.mean(x * x, axis=-1, keepdims=True) + EPS) * w


def _split3(x):
    hi = x.astype(BF16)
    r1 = x - hi.astype(F32)
    mid = r1.astype(BF16)
    lo = (r1 - mid.astype(F32)).astype(BF16)
    return hi, mid, lo


def _mod_kernel(c_ref, w_ref, b_ref, o_ref):
    c = c_ref[...]
    a = c * _sigmoid(c)
    a_hi = a.astype(BF16)
    a_lo = (a - a_hi.astype(F32)).astype(BF16)
    w = w_ref[0]
    w_hi = w.astype(BF16)
    w_lo = (w - w_hi.astype(F32)).astype(BF16)
    o_ref[0] = _dot(a_hi, w_hi) + _dot(a_lo, w_hi) + _dot(a_hi, w_lo) + b_ref[0]


def _modulation(c_all, w_mod, b_mod):
    depth, d, n = w_mod.shape
    rows = c_all.shape[0]
    tn = 1536
    return pl.pallas_call(
        _mod_kernel,
        name="adaln_mod",
        grid=(depth, n // tn),
        in_specs=[
            pl.BlockSpec((rows, d), lambda l, j: (0, 0)),
            pl.BlockSpec((1, d, tn), lambda l, j: (l, 0, j)),
            pl.BlockSpec((1, 1, tn), lambda l, j: (l, 0, j)),
        ],
        out_specs=pl.BlockSpec((1, rows, tn), lambda l, j: (l, 0, j)),
        out_shape=jax.ShapeDtypeStruct((depth, rows, n), F32),
        compiler_params=_cparams(("parallel", "parallel")),
    )(c_all, w_mod, b_mod.reshape(depth, 1, n))


def _h_specs(tm, d, n_lat_tiles, n_ctx_tiles, ctx_block0):
    return (pl.BlockSpec((tm, d), lambda i: (jnp.minimum(i, n_lat_tiles - 1), 0)),
            pl.BlockSpec((tm, d), lambda i: (ctx_block0 + jnp.clip(i - n_lat_tiles, 0, n_ctx_tiles - 1), 0)))


def _inproj_kernel(hl_ref, hc_ref, mod_ref, nw_ref, w_ref, o_ref, *, n_lat_tiles):
    h = jnp.where(pl.program_id(0) < n_lat_tiles, hl_ref[...], hc_ref[...])
    y = _rms(h, nw_ref[...])
    xn = (y * (1.0 + mod_ref[0, 1:2, :]) + mod_ref[0, 0:1, :]).astype(BF16)
    for c in range(0, Z_WIDTH, TN_IN):
        o_ref[:, c:c + TN_IN] = _dot(xn, w_ref[:, c:c + TN_IN]).astype(BF16)


def _layer_spec(arr, layer, **kw):
    tail = arr.shape[1:]
    return pl.BlockSpec((None,) + tail, lambda *_: (layer,) + (0,) * len(tail), **kw)


def _mod_spec(mod_all, layer, row_of_tile):
    return pl.BlockSpec((None, 1) + mod_all.shape[2:], lambda i: (layer, row_of_tile(i), 0, 0))


def _inproj(h_lat, h_ctx, ctx_block0, mod_all, norm_w, w_in_p, layer, batch):
    d = h_lat.shape[1]
    ntok = batch * (SEQ + CTX_LEN)
    tm = TM_TOK
    n_lat_tiles = batch * SEQ // tm
    n_ctx_tiles = batch * CTX_LEN // tm
    per_batch = SEQ // tm
    mod_row = lambda i: jnp.where(i < n_lat_tiles, i // per_batch, batch)
    return pl.pallas_call(
        functools.partial(_inproj_kernel, n_lat_tiles=n_lat_tiles),
        name="in_proj",
        grid=(ntok // tm,),
        in_specs=[
            *_h_specs(tm, d, n_lat_tiles, n_ctx_tiles, ctx_block0),
            _mod_spec(mod_all, layer, mod_row),
            _layer_spec(norm_w, layer),
            _layer_spec(w_in_p, layer, pipeline_mode=pl.Buffered(1)),
        ],
        out_specs=pl.BlockSpec((tm, Z_WIDTH), lambda i: (i, 0)),
        out_shape=jax.ShapeDtypeStruct((ntok, Z_WIDTH), BF16),
        compiler_params=_cparams(("parallel",)),
    )(h_lat, h_ctx, mod_all, norm_w, w_in_p)


def _mla_prep_kernel(aq_ref, akv_ref, akr_ref, qnw_ref, kvnw_ref, wq_ref, wkn_ref, wv_ref,
                     tqc_ref, tqs_ref, tkc_ref, tks_ref, q_out, k_out, v_out):
    qn = _rms(aq_ref[...].astype(F32), qnw_ref[...]).astype(BF16)
    qq = _dot(qn, wq_ref[...])
    tqc = tqc_ref[...]
    tqs = tqs_ref[...]
    for h in range(MLA_HEADS):
        blk = qq[:, h * LANE:(h + 1) * LANE]
        swapped = pltpu.roll(blk, LANE - MLA_ROPE, axis=1)
        q_out[:, h * LANE:(h + 1) * LANE] = (blk * tqc + swapped * tqs).astype(BF16)
    kvn = _rms(akv_ref[...].astype(F32), kvnw_ref[...]).astype(BF16)
    kn = _dot(kvn, wkn_ref[...])
    r = akr_ref[...].astype(F32)
    kr = r[:, :LANE] * tkc_ref[...] + r[:, LANE:] * tks_ref[...]
    for h in range(MLA_HEADS):
        lo = h * LANE
        k_out[:, lo:lo + LANE] = (kn[:, lo:lo + LANE] + kr).astype(BF16)
    v_out[...] = _dot(kvn, wv_ref[...]).astype(BF16)


def _mla_prep(z, qnw, kvnw, wq, wkn, wv, tabs, layer, n_lat_tiles):
    ntok = z.shape[0]
    tm = TM_IN
    per_batch = SEQ // tm
    tab_idx = lambda i: (jnp.where(i < n_lat_tiles, i % per_batch, per_batch), 0)
    hw = MLA_HEADS * LANE
    tab_spec = pl.BlockSpec((tm, LANE), tab_idx)
    return pl.pallas_call(
        _mla_prep_kernel,
        name="mla_prep",
        grid=(ntok // tm,),
        in_specs=[
            pl.BlockSpec((tm, MLA_Q_RANK), lambda i: (i, Z_AQ // MLA_Q_RANK)),
            pl.BlockSpec((tm, MLA_KV_RANK), lambda i: (i, Z_AKV // MLA_KV_RANK)),
            pl.BlockSpec((tm, 2 * LANE), lambda i: (i, Z_AKR // (2 * LANE))),
            _layer_spec(qnw, layer), _layer_spec(kvnw, layer),
            _layer_spec(wq, layer), _layer_spec(wkn, layer), _layer_spec(wv, layer),
            tab_spec, tab_spec, tab_spec, tab_spec,
        ],
        out_specs=[
            pl.BlockSpec((tm, hw), lambda i: (i, 0)),
            pl.BlockSpec((tm, hw), lambda i: (i, 0)),
            pl.BlockSpec((tm, MLA_HEADS * MLA_V), lambda i: (i, 0)),
        ],
        out_shape=[
            jax.ShapeDtypeStruct((ntok, hw), BF16),
            jax.ShapeDtypeStruct((ntok, hw), BF16),
            jax.ShapeDtypeStruct((ntok, MLA_HEADS * MLA_V), BF16),
        ],
        compiler_params=_cparams(("parallel",)),
    )(z, z, z, qnw, kvnw, wq, wkn, wv, *tabs)


def _attend_t(qs, n_chunks, chunk):
    if len(qs) > HEAD_GROUP:
        outs = []
        for g in range(0, len(qs), HEAD_GROUP):
            outs += _attend_t(qs[g:g + HEAD_GROUP], n_chunks, lambda h, c, g=g: chunk(g + h, c))
        return outs
    heads = range(len(qs))
    dv = chunk(0, 0)[1].shape[0]
    m = o = None

    def scores(c):
        parts = [chunk(h, c) for h in heads]
        s = [_dot_nt(parts[h][0], qs[h]) for h in heads]
        return parts, [s[h] if parts[h][2] is None else s[h] + parts[h][2] for h in heads]

    ahead =[scores(c) for c in range(min(SCORE_LOOKAHEAD, n_chunks))]
    for c in range(n_chunks):
        parts, s = ahead.pop(0)
        if c + SCORE_LOOKAHEAD < n_chunks:
            ahead.append(scores(c + SCORE_LOOKAHEAD))
        mc = [jnp.max(s[h], axis=0, keepdims=True) for h in heads]
        ones = jnp.ones((BF16_SUBLANES, parts[0][1].shape[1]), BF16)
        vt1 = [jnp.concatenate([parts[h][1], ones], axis=0) for h in heads]
        if m is None:
            m = mc
            o = [_dot(vt1[h], jnp.exp2(s[h] - m[h]).astype(BF16)) for h in heads]
        else:
            m_new = [jnp.maximum(m[h], mc[h]) for h in heads]
            alpha = [jnp.exp2(m[h] - m_new[h]) for h in heads]
            o = [o[h] * alpha[h] + _dot(vt1[h], jnp.exp2(s[h] - m_new[h]).astype(BF16)) for h in heads]
            m = m_new
    return [o[h][:dv] / o[h][dv:dv + 1] for h in heads]


def _transpose_rows(src_ref, dst_ref, rows):
    for r in range(0, rows, TQ):
        dst_ref[:, r:r + TQ] = src_ref[r:r + TQ, :].astype(F32).T.astype(BF16)


def _mla_attn_kernel(q_ref, kl_ref, kc_ref, vl_ref, vc_ref, o_ref, vlt_s, vct_s, *, n_lat_q):
    t = pl.program_id(1)

    @pl.when(t == 0)
    def _():
        _transpose_rows(vl_ref, vlt_s, SEQ)
        _transpose_rows(vc_ref, vct_s, CTX_LEN)

    def attend(with_lat):
        def chunk(h, c):
            cols = slice(h * LANE, (h + 1) * LANE)
            rows = slice(h * MLA_V, (h + 1) * MLA_V)
            if c == 0:
                return kc_ref[:, cols], vct_s[rows, :], None
            keys = slice((c - 1) * MLA_KEY_CHUNK, c * MLA_KEY_CHUNK)
            return kl_ref[keys, cols], vlt_s[rows, keys], None

        qs = [q_ref[:, h * LANE:(h + 1) * LANE] for h in range(MLA_HEADS)]
        outs = _attend_t(qs, 1 + (SEQ // MLA_KEY_CHUNK if with_lat else 0), chunk)
        o_ref[...] = jnp.concatenate(outs, axis=0).T.astype(BF16)

    pl.when(t < n_lat_q)(lambda: attend(True))
    pl.when(t >= n_lat_q)(lambda: attend(False))


def _q_row_block(n_lat_q, ctx_base):
    return lambda b, t: jnp.where(t < n_lat_q, b * n_lat_q + t, ctx_base + b)


def _mla_attn(q, k, v, batch, need_ctx):
    ntok = q.shape[0]
    n_lat_q = SEQ // TQ
    ctx_base = batch * SEQ // CTX_LEN
    qrow = _q_row_block(n_lat_q, ctx_base)
    hw = MLA_HEADS * LANE
    vw = MLA_HEADS * MLA_V
    return pl.pallas_call(
        functools.partial(_mla_attn_kernel, n_lat_q=n_lat_q),
        name="mla_attn",
        grid=(batch, n_lat_q + (1 if need_ctx else 0)),
        in_specs=[
            pl.BlockSpec((TQ, hw), lambda b, t: (qrow(b, t), 0)),
            pl.BlockSpec((SEQ, hw), lambda b, t: (b, 0)),
            pl.BlockSpec((CTX_LEN, hw), lambda b, t: (ctx_base + b, 0)),
            pl.BlockSpec((SEQ, vw), lambda b, t: (b, 0)),
            pl.BlockSpec((CTX_LEN, vw), lambda b, t: (ctx_base + b, 0)),
        ],
        out_specs=pl.BlockSpec((TQ, vw), lambda b, t: (qrow(b, t), 0)),
        out_shape=jax.ShapeDtypeStruct((ntok, vw), BF16),
        scratch_shapes=[pltpu.VMEM((vw, SEQ), BF16), pltpu.VMEM((vw, CTX_LEN), BF16)],
        compiler_params=_cparams(("parallel", "arbitrary")),
    )(q, k, k, v, v)


def _gla_kernel(ql_ref, qc_ref, kl_ref, kc_ref, vl_ref, vc_ref, gl_ref, gc_ref, lrl_ref, lrc_ref,
                wg_ref, bg_ref, nw_ref, yl_ref, yc_ref,
                v_s, qd_s, ke_s, dec_s, o_s, st_s):
    n_tok = SEQ + CTX_LEN
    n_chunk = n_tok // GLA_CHUNK
    ctx_chunks = CTX_LEN // GLA_CHUNK
    ck = GLA_CHUNK
    tile = GLA_TILE
    per_tile = tile // ck
    qk_w = GLA_QK_W
    pair_k = GLA_PAIR * GLA_DK
    pair_v = GLA_PAIR * GLA_DV
    n_pairs = GLA_HEADS // GLA_PAIR

    v_s[0:SEQ, :] = vl_ref[...]
    v_s[SEQ:n_tok, :] = vc_ref[...]
    st_s[...] = jnp.zeros_like(st_s)

    row = lax.broadcasted_iota(jnp.int32, (tile, tile), 0)
    col = lax.broadcasted_iota(jnp.int32, (tile, tile), 1)
    same_chunk = (row // ck) == (col // ck)
    keep = (same_chunk & (col <= row), same_chunk & (col >= row))
    tri3 = tuple(jnp.concatenate([jnp.where(kp, 1.0, 0.0).astype(BF16)] * 3, axis=1) for kp in keep)
    q_head = lax.broadcasted_iota(jnp.int32, (tile, qk_w), 1) // GLA_DK
    bd_pair = (lax.broadcasted_iota(jnp.int32, (pair_k, pair_v), 0) // GLA_DK
               == lax.broadcasted_iota(jnp.int32, (pair_k, pair_v), 1) // GLA_DV)

    def log_decay(lr):
        zz = _dot(lr, wg_ref[...]) + bg_ref[...]
        return (jnp.minimum(zz, 0.0) - jnp.log1p(jnp.exp(-jnp.abs(zz)))) * (1.0 / GLA_GATE_TAU)

    def prologue(tiles):
        nt = range(len(tiles))
        dirs = range(2)
        heads = range(GLA_HEADS)
        rows = [pl.ds(tiles[t][4], tile) for t in nt]
        q = [tiles[t][0].astype(F32) * (GLA_DK ** -0.5) for t in nt]
        k = [tiles[t][1].astype(F32) for t in nt]
        la = [log_decay(tiles[t][3]) for t in nt]
        cum = [[_dot(tri3[d], jnp.concatenate(_split3(la[t][:, d * qk_w:(d + 1) * qk_w]), axis=0))
                for d in dirs] for t in nt]
        edge = (ck - 1, 0)
        lasts = [[[cum[t][d][c * ck + edge[d]:c * ck + edge[d] + 1, :] for c in range(per_tile)]
                  for d in dirs] for t in nt]
        last = [[jnp.concatenate([jnp.broadcast_to(x, (ck, qk_w)) for x in lasts[t][d]], axis=0)
                 for d in dirs] for t in nt]
        q_dec = [[(q[t] * jnp.exp(cum[t][d])).astype(BF16) for d in dirs] for t in nt]
        k_inv = [[(k[t] * jnp.exp(-cum[t][d])).astype(BF16) for d in dirs] for t in nt]
        for t in nt:
            for d in dirs:
                qd_s[d, rows[t], :] = q_dec[t][d]
                ke_s[d, rows[t], :] = (k[t] * jnp.exp(last[t][d] - cum[t][d])).astype(BF16)
                for c in range(per_tile):
                    dec_s[d, tiles[t][4] // ck + c] = jnp.broadcast_to(jnp.exp(lasts[t][d][c]), (8, qk_w))
        qm = [[[jnp.where(q_head == h, q_dec[t][d], jnp.zeros((), BF16)) for h in heads] for d in dirs] for t in nt]
        s = [[[_dot_nt(qm[t][d][h], k_inv[t][d]) for h in heads] for d in dirs] for t in nt]
        att = [[[jnp.where(keep[d], s[t][d][h], 0.0).astype(BF16) for h in heads] for d in dirs] for t in nt]
        o = [[[_dot(att[t][d][h], tiles[t][2][:, h * GLA_DV:(h + 1) * GLA_DV]) for h in heads]
              for d in dirs] for t in nt]
        for t in nt:
            for d in dirs:
                for h in heads:
                    o_s[d, rows[t], h * GLA_DV:(h + 1) * GLA_DV] = o[t][d][h]

    def lat_tiles(i, carry):
        tiles = []
        for j in range(GLA_TILES_PER_STEP):
            r0 = pl.multiple_of((i * GLA_TILES_PER_STEP + j) * tile, tile)
            rows = pl.ds(r0, tile)
            tiles.append((ql_ref[rows, :], kl_ref[rows, :], vl_ref[rows, :], lrl_ref[rows, :], r0))
        prologue(tiles)
        return carry

    lax.fori_loop(0, SEQ // (tile * GLA_TILES_PER_STEP), lat_tiles, 0)
    for r0 in range(0, CTX_LEN, tile):
        rows = slice(r0, r0 + tile)
        prologue([(qc_ref[rows, :], kc_ref[rows, :], vc_ref[rows, :], lrc_ref[rows, :], SEQ + r0)])

    def scan_step(n, d):
        rows = pl.ds(pl.multiple_of(n * ck, ck), ck)
        dec = dec_s[d, n]
        for p in range(n_pairs):
            kcols = slice(p * pair_k, (p + 1) * pair_k)
            vcols = slice(p * pair_v, (p + 1) * pair_v)
            st = st_s[d, p]
            o_s[d, rows, vcols] = o_s[d, rows, vcols] + _dot(qd_s[d, rows, kcols], st.astype(BF16))
            kv = _dot_tn(ke_s[d, rows, kcols], v_s[rows, vcols])
            decay_rows = jnp.broadcast_to(dec[0:1, kcols], (pair_v, pair_k)).T
            st_s[d, p] = st * decay_rows + jnp.where(bd_pair, kv, 0.0)

    def body(i, carry):
        nf = jnp.where(i < ctx_chunks, n_chunk - ctx_chunks + i, i - ctx_chunks)
        nb = n_chunk - 1 - i
        scan_step(nf, 0)
        scan_step(nb, 1)
        return carry

    lax.fori_loop(0, n_chunk, body, 0, unroll=2)

    nw = nw_ref[...]

    def finish(r0, rows, g):
        o = o_s[0, r0:r0 + rows, :] + o_s[1, r0:r0 + rows, :]
        g = g.astype(F32)
        parts = []
        for h in range(GLA_HEADS):
            parts.append(_rms(o[:, h * GLA_DV:(h + 1) * GLA_DV], nw))
        y = jnp.concatenate(parts, axis=1)
        return (y * (g * _sigmoid(g))).astype(BF16)

    rt = GLA_TILE
    for i in range(SEQ // rt):
        yl_ref[i * rt:(i + 1) * rt, :] = finish(i * rt, rt, gl_ref[i * rt:(i + 1) * rt, :])
    yc_ref[...] = finish(SEQ, CTX_LEN, gc_ref[...])


def _gla(z, wg, bg, nw, layer, batch):
    ctx_base = batch * SEQ // CTX_LEN
    n_tok = SEQ + CTX_LEN

    def lat(width, col0):
        return pl.BlockSpec((SEQ, width), lambda b: (b, col0 // width))

    def ctx(width, col0):
        return pl.BlockSpec((CTX_LEN, width), lambda b: (ctx_base + b, col0 // width))

    return pl.pallas_call(
        _gla_kernel,
        name="gla_scan",
        grid=(batch,),
        in_specs=[
            lat(GLA_QK_W, Z_BQ), ctx(GLA_QK_W, Z_BQ),
            lat(GLA_QK_W, Z_BK), ctx(GLA_QK_W, Z_BK),
            lat(GLA_V_W, Z_BV), ctx(GLA_V_W, Z_BV),
            lat(GLA_V_W, Z_BG), ctx(GLA_V_W, Z_BG),
            lat(LANE, Z_LR), ctx(LANE, Z_LR),
            _layer_spec(wg, layer), _layer_spec(bg, layer), _layer_spec(nw, layer),
        ],
        out_specs=[
            pl.BlockSpec((SEQ, GLA_V_W), lambda b: (b, 0)),
            pl.BlockSpec((CTX_LEN, GLA_V_W), lambda b: (b, 0)),
        ],
        out_shape=[
            jax.ShapeDtypeStruct((batch * SEQ, GLA_V_W), BF16),
            jax.ShapeDtypeStruct((batch * CTX_LEN, GLA_V_W), BF16),
        ],
        scratch_shapes=[
            pltpu.VMEM((n_tok, GLA_V_W), BF16),
            pltpu.VMEM((2, n_tok, GLA_QK_W), BF16),
            pltpu.VMEM((2, n_tok, GLA_QK_W), BF16),
            pltpu.VMEM((2, n_tok // GLA_CHUNK, 8, GLA_QK_W), F32),
            pltpu.VMEM((2, n_tok, GLA_V_W), F32),
            pltpu.VMEM((2, GLA_HEADS // GLA_PAIR, GLA_PAIR * GLA_DK, GLA_PAIR * GLA_DV), F32),
        ],
        compiler_params=_cparams(("parallel",)),
    )(z, z, z, z, z, z, z, z, z, z, wg, bg, nw)


def _na_kernel(q_ref, kl_ref, kc_ref, vl_ref, vc_ref, bias_ref, o_ref, vlt_s, vct_s, *, n_lat_q):
    lane = lax.broadcasted_iota(jnp.int32, (TQ, LANE), 1)
    t = pl.program_id(1)

    @pl.when(t == 0)
    def _():
        _transpose_rows(vl_ref, vlt_s, SEQ)
        _transpose_rows(vc_ref, vct_s, CTX_LEN)

    def attend(with_band):
        if with_band:
            r0 = jnp.clip(NA_QROWS * t - NA_KH // 2, 0, GRID_ROWS - NA_BAND_ROWS)
            start = pl.multiple_of(r0 * GRID_W, NA_QBLK)
            kind = jnp.where(t == 0, 0, jnp.where(t == n_lat_q - 1, 2, 1))

        def chunk(h, c):
            cols = slice((h // 2) * LANE, (h // 2 + 1) * LANE)
            rows = slice(h * NA_HEAD_DIM, (h + 1) * NA_HEAD_DIM)
            if c == 0:
                return kc_ref[:, cols], vct_s[rows, :], None
            off = (c - 1) * KEY_CHUNK
            keys = pl.ds(pl.multiple_of(start + off, KEY_CHUNK), KEY_CHUNK)
            return kl_ref[keys, cols], vlt_s[rows, keys], bias_ref[kind, h, off:off + KEY_CHUNK, :]

        qs = []
        for h in range(NA_HEADS):
            head_lanes = lane < NA_HEAD_DIM if h % 2 == 0 else lane >= NA_HEAD_DIM
            qs.append(jnp.where(head_lanes, q_ref[:, (h // 2) * LANE:(h // 2 + 1) * LANE], jnp.zeros((), BF16)))
        outs = _attend_t(qs, 1 + (NA_BAND // KEY_CHUNK if with_band else 0), chunk)
        o_ref[...] = jnp.concatenate(outs, axis=0).T.astype(BF16)

    pl.when(t < n_lat_q)(lambda: attend(True))
    pl.when(t >= n_lat_q)(lambda: attend(False))


def _na(z, bias, layer, batch, need_ctx):
    ntok = z.shape[0]
    n_lat_q = SEQ // TQ
    ctx_base = batch * SEQ // CTX_LEN
    qrow = _q_row_block(n_lat_q, ctx_base)
    w = NA_W
    return pl.pallas_call(
        functools.partial(_na_kernel, n_lat_q=n_lat_q),
        name="na_attn",
        grid=(batch, n_lat_q + (1 if need_ctx else 0)),
        in_specs=[
            pl.BlockSpec((TQ, w), lambda b, t: (qrow(b, t), Z_CQ // w)),
            pl.BlockSpec((SEQ, w), lambda b, t: (b, Z_CK // w)),
            pl.BlockSpec((CTX_LEN, w), lambda b, t: (ctx_base + b, Z_CK // w)),
            pl.BlockSpec((SEQ, w), lambda b, t: (b, Z_CV // w)),
            pl.BlockSpec((CTX_LEN, w), lambda b, t: (ctx_base + b, Z_CV // w)),
            _layer_spec(bias, layer, pipeline_mode=pl.Buffered(1)),
        ],
        out_specs=pl.BlockSpec((TQ, w), lambda b, t: (qrow(b, t), 0)),
        out_shape=jax.ShapeDtypeStruct((ntok, w), BF16),
        scratch_shapes=[pltpu.VMEM((w, SEQ), BF16), pltpu.VMEM((w, CTX_LEN), BF16)],
        compiler_params=_cparams(("parallel", "arbitrary")),
    )(z, z, z, z, z, bias)


def _merge_kernel(hl_ref, hc_ref, ya_ref, ybl_ref, ybc_ref, yc_ref, ga_ref, gb_ref, gc_ref, mod_ref,
                  wa_ref, wb_ref, wc_ref, wo_ref, o_ref, *, n_lat_tiles):
    is_lat = pl.program_id(0) < n_lat_tiles
    h = jnp.where(is_lat, hl_ref[...], hc_ref[...])
    yb = jnp.where(is_lat, ybl_ref[...], ybc_ref[...])
    m = (_sigmoid(ga_ref[...].astype(F32)) * _dot(ya_ref[...], wa_ref[...])
         + _sigmoid(gb_ref[...].astype(F32)) * _dot(yb, wb_ref[...])
         + _sigmoid(gc_ref[...].astype(F32)) * _dot(yc_ref[...], wc_ref[...]))
    o_ref[...] = h + mod_ref[0, 2:3, :] * _dot(m.astype(BF16), wo_ref[...])


def _merge(h_lat, h_ctx, ctx_block0, ya, yb_lat, yb_ctx, yc, z, mod_all, wa, wb, wc, wo, layer, batch, need_ctx):
    d = h_lat.shape[1]
    ntok = batch * (SEQ + CTX_LEN)
    tm = TM_TOK
    n_lat_tiles = batch * SEQ // tm
    n_tiles = ntok // tm if need_ctx else n_lat_tiles
    per_batch = SEQ // tm
    bw = wa.shape[1]
    tok = lambda width: pl.BlockSpec((tm, width), lambda i: (i, 0))
    gate = lambda g: pl.BlockSpec((tm, d), lambda i: (i, Z_GATES // d + g))
    return pl.pallas_call(
        functools.partial(_merge_kernel, n_lat_tiles=n_lat_tiles),
        name="merge",
        grid=(n_tiles,),
        in_specs=[
            *_h_specs(tm, d, n_lat_tiles, batch * CTX_LEN // tm, ctx_block0), tok(bw),
            pl.BlockSpec((tm, bw), lambda i: (jnp.minimum(i, n_lat_tiles - 1), 0)),
            pl.BlockSpec((tm, bw), lambda i: (jnp.maximum(i - n_lat_tiles, 0), 0)),
            tok(bw), gate(0), gate(1), gate(2),
            _mod_spec(mod_all, layer, lambda i: jnp.where(i < n_lat_tiles, i // per_batch, batch)),
            _layer_spec(wa, layer), _layer_spec(wb, layer), _layer_spec(wc, layer), _layer_spec(wo, layer),
        ],
        out_specs=tok(d),
        out_shape=jax.ShapeDtypeStruct((ntok, d), F32),
        compiler_params=_cparams(("parallel",)),
    )(h_lat, h_ctx, ya, yb_lat, yb_ctx, yc, z, z, z, mod_all, wa, wb, wc, wo)


def _ffn_kernel(h_ref, mod_ref, nw_ref, win_ref, wout_ref, fw_ref, o_ref, *, final):
    h = h_ref[...]
    u = (_rms(h, nw_ref[...]) * (1.0 + mod_ref[0, 4:5, :]) + mod_ref[0, 3:4, :]).astype(BF16)
    acc = None
    for lo, hi in FFN_CHUNKS:
        g = _dot(u, win_ref[:, lo:hi])
        up = _dot(u, win_ref[:, FFN_HIDDEN + lo:FFN_HIDDEN + hi])
        act = (g * _sigmoid(g) * up).astype(BF16)
        part = _dot(act, wout_ref[lo:hi, :])
        acc = part if acc is None else acc + part
    out = h + mod_ref[0, 5:6, :] * acc
    if final:
        out = _rms(out, fw_ref[...])
    o_ref[...] = out


def _ffn(h, mod_all, norm_w, w_in, w_out, final_w, layer, batch, need_ctx, final):
    ntok, d = h.shape
    tm = TM_TOK
    n_lat_tiles = batch * SEQ // tm
    n_tiles = ntok // tm if need_ctx else n_lat_tiles
    per_batch = SEQ // tm
    resident = dict(pipeline_mode=pl.Buffered(1))
    out_rows = n_tiles * tm if final else ntok
    return pl.pallas_call(
        functools.partial(_ffn_kernel, final=final),
        name="ffn",
        grid=(n_tiles,),
        in_specs=[
            pl.BlockSpec((tm, d), lambda i: (i, 0)),
            _mod_spec(mod_all, layer, lambda i: jnp.where(i < n_lat_tiles, i // per_batch, batch)),
            _layer_spec(norm_w, layer),
            _layer_spec(w_in, layer, **resident),
            _layer_spec(w_out, layer, **resident),
            pl.BlockSpec((1, d), lambda i: (0, 0)),
        ],
        out_specs=pl.BlockSpec((tm, d), lambda i: (i, 0)),
        out_shape=jax.ShapeDtypeStruct((out_rows, d), F32),
        compiler_params=_cparams(("parallel",)),
    )(h, mod_all, norm_w, w_in, w_out, final_w)


def _pad_cols(w, width):
    return jnp.pad(w, ((0, 0), (0, 0), (0, width - w.shape[-1])))


def _prep_w_in(w_in):
    a0 = 0
    b0 = MLA_Q_RANK + MLA_KV_RANK + MLA_ROPE
    c0 = b0 + 2 * GLA_QK_W + 2 * GLA_V_W + 2 * GLA_GATE_RANK
    g0 = c0 + 3 * NA_W
    sl = lambda lo, n: w_in[:, :, lo:lo + n]
    aq = sl(a0, MLA_Q_RANK)
    akv = sl(a0 + MLA_Q_RANK, MLA_KV_RANK)
    kr0 = a0 + MLA_Q_RANK + MLA_KV_RANK
    half = MLA_ROPE // 2
    kr = sl(kr0, MLA_ROPE)
    kr_sw = jnp.concatenate([sl(kr0 + half, half), sl(kr0, half)], axis=-1)
    zeros = lambda n: jnp.zeros(w_in.shape[:2] + (n,), w_in.dtype)
    akr = jnp.concatenate([zeros(MLA_NOPE), kr, zeros(LANE - MLA_NOPE - MLA_ROPE),
                           zeros(MLA_NOPE), kr_sw, zeros(LANE - MLA_NOPE - MLA_ROPE)], axis=-1)
    bq = sl(b0, GLA_QK_W)
    bk = sl(b0 + GLA_QK_W, GLA_QK_W)
    bv = sl(b0 + 2 * GLA_QK_W, GLA_V_W)
    bg = sl(b0 + 2 * GLA_QK_W + GLA_V_W, GLA_V_W)
    lr = _pad_cols(sl(b0 + 2 * GLA_QK_W + 2 * GLA_V_W, 2 * GLA_GATE_RANK), LANE)
    cq = sl(c0, NA_W) * (NA_SCALE * LOG2E)
    ck = sl(c0 + NA_W, NA_W)
    cv = sl(c0 + 2 * NA_W, NA_W)
    gates = sl(g0, 3 * D_MODEL)
    out = jnp.concatenate([aq, lr, cq, ck, cv, bv, bg, bq, bk, akv, akr, gates], axis=-1)
    assert out.shape[-1] == Z_WIDTH
    return out.astype(BF16)


def _prep_mla_weights(w_q_up, w_kv_up):
    depth = w_q_up.shape[0]
    hd = MLA_NOPE + MLA_ROPE
    half = MLA_ROPE // 2
    wq = w_q_up.reshape(depth, MLA_Q_RANK, MLA_HEADS, hd)
    assert LANE - hd == MLA_ROPE
    wq_p = jnp.concatenate([wq, wq[..., MLA_NOPE + half:], wq[..., MLA_NOPE:MLA_NOPE + half]],
                           axis=-1).reshape(depth, MLA_Q_RANK, -1)
    wkv = w_kv_up.reshape(depth, MLA_KV_RANK, MLA_HEADS, MLA_NOPE + MLA_V)
    wkn = jnp.concatenate([wkv[..., :MLA_NOPE],
                           jnp.zeros((depth, MLA_KV_RANK, MLA_HEADS, LANE - MLA_NOPE), w_kv_up.dtype)], axis=-1)
    wv = wkv[..., MLA_NOPE:]
    return (wq_p.astype(BF16), wkn.reshape(depth, MLA_KV_RANK, -1).astype(BF16),
            wv.reshape(depth, MLA_KV_RANK, -1).astype(BF16))


def _rope_tables():
    t = np.arange(SEQ)
    rows = (t // GRID_W).astype(np.float32)
    cols = (t % GRID_W).astype(np.float32)
    n_freq = MLA_ROPE // 4
    inv_freq = jnp.asarray(ROPE_BASE, F32) ** (-jnp.arange(n_freq, dtype=F32) / n_freq)
    ang = jnp.concatenate([jnp.asarray(rows)[:, None] * inv_freq, jnp.asarray(cols)[:, None] * inv_freq], axis=-1)
    cos, sin = jnp.cos(ang), jnp.sin(ang)
    cos = jnp.concatenate([cos, jnp.ones((TM_IN, MLA_ROPE // 2), F32)], axis=0)
    sin = jnp.concatenate([sin, jnp.zeros((TM_IN, MLA_ROPE // 2), F32)], axis=0)
    n = cos.shape[0]
    tail = jnp.zeros((n, LANE - MLA_NOPE - MLA_ROPE), F32)
    c_rot = jnp.concatenate([cos, cos], axis=-1)
    s_rot = jnp.concatenate([-sin, sin], axis=-1)
    nope1 = jnp.ones((n, MLA_NOPE), F32)
    nope0 = jnp.zeros((n, MLA_NOPE), F32)
    tqc = jnp.concatenate([nope1, c_rot, tail], axis=-1) * (MLA_SCALE * LOG2E)
    tqs = jnp.concatenate([nope0, s_rot, tail], axis=-1) * (MLA_SCALE * LOG2E)
    tkc = jnp.concatenate([nope0, c_rot, tail], axis=-1)
    tks = jnp.concatenate([nope0, s_rot, tail], axis=-1)
    return tqc, tqs, tkc, tks


def _na_bias_tables(rpb):
    depth = rpb.shape[0]
    n_off = 2 * NA_KH - 1
    w2 = 2 * GRID_W
    lo = GRID_W - NA_KW
    u = jnp.pad(rpb.astype(F32), ((0, 0), (0, 0), (0, 0), (lo, w2 - lo - (2 * NA_KW - 1))))
    skew = jnp.tile(u, (1, 1, 1, GRID_W))[..., :GRID_W * (w2 - 1)].reshape(depth, NA_HEADS, n_off, GRID_W, w2 - 1)
    toep = skew[..., GRID_W - 1:]
    col = np.arange(GRID_W)
    cs = np.clip(col - NA_KW // 2, 0, GRID_W - NA_KW)
    col_ok = (col[None, :] >= cs[:, None]) & (col[None, :] < cs[:, None] + NA_KW)
    toep = jnp.swapaxes(jnp.where(jnp.asarray(col_ok), toep * LOG2E, -1e30), -1, -2)
    masked = jnp.full((depth, NA_HEADS, GRID_W, GRID_W), -1e30, F32)
    n_blocks = GRID_ROWS // NA_QROWS
    tables = []
    for g in (0, 1, n_blocks - 1):
        band0 = int(np.clip(NA_QROWS * g - NA_KH // 2, 0, GRID_ROWS - NA_BAND_ROWS))
        k_rows = []
        for i in range(NA_BAND_ROWS):
            kr = band0 + i
            blocks = []
            for j in range(NA_QROWS):
                qr = NA_QROWS * g + j
                r0 = int(np.clip(qr - NA_KH // 2, 0, GRID_ROWS - NA_KH))
                blocks.append(toep[:, :, kr - qr + NA_KH - 1] if r0 <= kr < r0 + NA_KH else masked)
            k_rows.append(jnp.concatenate(blocks, axis=-1))
        tables.append(jnp.concatenate(k_rows, axis=-2))
    return jnp.stack(tables, axis=1)


def _prep_gla_gate(w_f, b_f, w_b, b_b):
    depth = w_f.shape[0]
    wg = jnp.zeros((depth, LANE, 2 * GLA_QK_W), F32)
    wg = wg.at[:, :GLA_GATE_RANK, :GLA_QK_W].set(w_f)
    wg = wg.at[:, GLA_GATE_RANK:2 * GLA_GATE_RANK, GLA_QK_W:].set(w_b)
    bg = jnp.concatenate([b_f, b_b], axis=-1)[:, None, :]
    return wg.astype(BF16), bg


def kernel(x, c, ctx, c_ctx, w_mod, b_mod, norm1_w, w_in, mla_q_norm_w, mla_kv_norm_w, mla_w_q_up, mla_w_kv_up, gla_w_gate_f, gla_b_gate_f, gla_w_gate_b, gla_b_gate_b, gla_norm_w, na_rpb, w_a_o, w_b_o, w_c_o, w_out, norm2_w, w_ffn_in, w_ffn_out, final_norm_w):
    batch, seq, d = x.shape
    depth = w_mod.shape[0]
    assert seq == SEQ and d == D_MODEL and ctx.shape[1] == CTX_LEN
    assert (batch * CTX_LEN) % TM_IN == 0
    n_lat_in = batch * SEQ // TM_IN

    mod_rows = -(-(batch + 1) // 8) * 8
    c_all = jnp.concatenate([c, c_ctx[None, :], jnp.zeros((mod_rows - batch - 1, d), F32)], axis=0)
    mod_all = _modulation(c_all, w_mod, b_mod).reshape(depth, mod_rows, 6, d)

    w_in_p = _prep_w_in(w_in)
    wq_p, wkn_p, wv_p = _prep_mla_weights(mla_w_q_up, mla_w_kv_up)
    tabs = _rope_tables()
    na_bias = _na_bias_tables(na_rpb)
    wg_p, bg_p = _prep_gla_gate(gla_w_gate_f, gla_b_gate_f, gla_w_gate_b, gla_b_gate_b)
    wa, wb, wc, wo = (w.astype(BF16) for w in (w_a_o, w_b_o, w_c_o, w_out))
    wfi, wfo = w_ffn_in.astype(BF16), w_ffn_out.astype(BF16)
    n1, n2, qnw, kvnw, gnw = (w[:, None, :] for w in (norm1_w, norm2_w, mla_q_norm_w, mla_kv_norm_w, gla_norm_w))

    hs = (x.reshape(batch * SEQ, d), ctx.reshape(batch * CTX_LEN, d), 0)
    for i in range(depth):
        need_ctx = i < depth - 1
        z = _inproj(*hs, mod_all, n1, w_in_p, i, batch)
        q, k, v = _mla_prep(z, qnw, kvnw, wq_p, wkn_p, wv_p, tabs, i, n_lat_in)
        ya = _mla_attn(q, k, v, batch, need_ctx)
        yb_lat, yb_ctx = _gla(z, wg_p, bg_p, gnw, i, batch)
        yc = _na(z, na_bias, i, batch, need_ctx)
        h = _merge(*hs, ya, yb_lat, yb_ctx, yc, z, mod_all, wa, wb, wc, wo, i, batch, need_ctx)
        h = _ffn(h, mod_all, n2, wfi, wfo, final_norm_w[None, :], i, batch, need_ctx, not need_ctx)
        hs = (h, h, batch * SEQ // TM_TOK)
    return h.reshape(batch, SEQ, d)
```

```python
import functools

import jax
import jax.numpy as jnp
import numpy as np
from jax import lax
from jax.experimental import pallas as pl
from jax.experimental.pallas import tpu as pltpu

F32 = jnp.float32
BF16 = jnp.bfloat16

D_MODEL = 1024
SEQ = 2048
CTX_LEN = 256
GRID_W = 64
GRID_ROWS = SEQ // GRID_W
EPS = 1e-6
ROPE_BASE = 10000.0

MLA_HEADS = 8
MLA_Q_RANK = 384
MLA_KV_RANK = 256
MLA_NOPE = 64
MLA_ROPE = 32
MLA_V = 64
MLA_SCALE = (MLA_NOPE + MLA_ROPE) ** -0.5

GLA_HEADS = 4
GLA_DK = 64
GLA_DV = 128
GLA_GATE_RANK = 16
GLA_GATE_TAU = 16.0
GLA_CHUNK = 64
GLA_TILE = 256
GLA_TILES_PER_STEP = 2
GLA_PAIR = 2
GLA_QK_W = GLA_HEADS * GLA_DK
GLA_V_W = GLA_HEADS * GLA_DV

NA_HEADS = 8
NA_HEAD_DIM = 64
NA_KH = 8
NA_KW = 16
NA_W = NA_HEADS * NA_HEAD_DIM
NA_SCALE = NA_HEAD_DIM ** -0.5
LOG2E = 1.4426950408889634
NA_QROWS = 4
NA_BAND_ROWS = NA_KH + NA_QROWS
NA_BAND = NA_BAND_ROWS * GRID_W
NA_QBLK = NA_QROWS * GRID_W

FFN_HIDDEN = 2816
MXU_DIM = 256
_FFN_SPLIT = (FFN_HIDDEN // MXU_DIM + 1) // 2 * MXU_DIM
FFN_CHUNKS = ((0, _FFN_SPLIT), (_FFN_SPLIT, FFN_HIDDEN))
LANE = 128
BF16_SUBLANES = 16

Z_AQ = 0
Z_LR = 384
Z_CQ = 512
Z_CK = 1024
Z_CV = 1536
Z_BV = 2048
Z_BG = 2560
Z_BQ = 3072
Z_BK = 3328
Z_AKV = 3584
Z_AKR = 3840
Z_GATES = 4096
Z_WIDTH = 7168

TM_IN = 1024
TN_IN = 1024
TM_TOK = 512
TQ = 256
KEY_CHUNK = 256
MLA_KEY_CHUNK = 256
SCORE_LOOKAHEAD = 2
HEAD_GROUP = 8
VMEM_LIMIT = 56 * 1024 * 1024


def _cparams(sem):
    return pltpu.CompilerParams(dimension_semantics=sem, vmem_limit_bytes=VMEM_LIMIT)


def _dot(a, b):
    return jnp.dot(a, b, preferred_element_type=F32)


def _dot_nt(a, b):
    return lax.dot_general(a, b, (((1,), (1,)), ((), ())), preferred_element_type=F32)


def _dot_tn(a, b):
    return lax.dot_general(a, b, (((0,), (0,)), ((), ())), preferred_element_type=F32)


def _sigmoid(x):
    return 0.5 * jnp.tanh(0.5 * x) + 0.5


def _rms(x, w):
    return x * lax.rsqrt(jnp.mean(x * x, axis=-1, keepdims=True) + EPS) * w


def _split3(x):
    hi = x.astype(BF16)
    r1 = x - hi.astype(F32)
    mid = r1.astype(BF16)
    lo = (r1 - mid.astype(F32)).astype(BF16)
    return hi, mid, lo


def _mod_kernel(c_ref, w_ref, b_ref, o_ref):
    c = c_ref[...]
    a = c * _sigmoid(c)
    a_hi = a.astype(BF16)
    a_lo = (a - a_hi.astype(F32)).astype(BF16)
    w = w_ref[0]
    w_hi = w.astype(BF16)
    w_lo = (w - w_hi.astype(F32)).astype(BF16)
    o_ref[0] = _dot(a_hi, w_hi) + _dot(a_lo, w_hi) + _dot(a_hi, w_lo) + b_ref[0]


def _modulation(c_all, w_mod, b_mod):
    depth, d, n = w_mod.shape
    rows = c_all.shape[0]
    tn = 1536
    return pl.pallas_call(
        _mod_kernel,
        name="adaln_mod",
        grid=(depth, n // tn),
        in_specs=[
            pl.BlockSpec((rows, d), lambda l, j: (0, 0)),
            pl.BlockSpec((1, d, tn), lambda l, j: (l, 0, j)),
            pl.BlockSpec((1, 1, tn), lambda l, j: (l, 0, j)),
        ],
        out_specs=pl.BlockSpec((1, rows, tn), lambda l, j: (l, 0, j)),
        out_shape=jax.ShapeDtypeStruct((depth, rows, n), F32),
        compiler_params=_cparams(("parallel", "parallel")),
    )(c_all, w_mod, b_mod.reshape(depth, 1, n))


def _h_specs(tm, d, n_lat_tiles, n_ctx_tiles, ctx_block0):
    return (pl.BlockSpec((tm, d), lambda i: (jnp.minimum(i, n_lat_tiles - 1), 0)),
            pl.BlockSpec((tm, d), lambda i: (ctx_block0 + jnp.clip(i - n_lat_tiles, 0, n_ctx_tiles - 1), 0)))


def _inproj_kernel(hl_ref, hc_ref, mod_ref, nw_ref, w_ref, o_ref, *, n_lat_tiles):
    h = jnp.where(pl.program_id(0) < n_lat_tiles, hl_ref[...], hc_ref[...])
    y = _rms(h, nw_ref[...])
    xn = (y * (1.0 + mod_ref[0, 1:2, :]) + mod_ref[0, 0:1, :]).astype(BF16)
    for c in range(0, Z_WIDTH, TN_IN):
        o_ref[:, c:c + TN_IN] = _dot(xn, w_ref[:, c:c + TN_IN]).astype(BF16)


def _layer_spec(arr, layer, **kw):
    tail = arr.shape[1:]
    return pl.BlockSpec((None,) + tail, lambda *_: (layer,) + (0,) * len(tail), **kw)


def _mod_spec(mod_all, layer, row_of_tile):
    return pl.BlockSpec((None, 1) + mod_all.shape[2:], lambda i: (layer, row_of_tile(i), 0, 0))


def _inproj(h_lat, h_ctx, ctx_block0, mod_all, norm_w, w_in_p, layer, batch):
    d = h_lat.shape[1]
    ntok = batch * (SEQ + CTX_LEN)
    tm = TM_TOK
    n_lat_tiles = batch * SEQ // tm
    n_ctx_tiles = batch * CTX_LEN // tm
    per_batch = SEQ // tm
    mod_row = lambda i: jnp.where(i < n_lat_tiles, i // per_batch, batch)
    return pl.pallas_call(
        functools.partial(_inproj_kernel, n_lat_tiles=n_lat_tiles),
        name="in_proj",
        grid=(ntok // tm,),
        in_specs=[
            *_h_specs(tm, d, n_lat_tiles, n_ctx_tiles, ctx_block0),
            _mod_spec(mod_all, layer, mod_row),
            _layer_spec(norm_w, layer),
            _layer_spec(w_in_p, layer, pipeline_mode=pl.Buffered(1)),
        ],
        out_specs=pl.BlockSpec((tm, Z_WIDTH), lambda i: (i, 0)),
        out_shape=jax.ShapeDtypeStruct((ntok, Z_WIDTH), BF16),
        compiler_params=_cparams(("parallel",)),
    )(h_lat, h_ctx, mod_all, norm_w, w_in_p)


def _mla_prep_kernel(aq_ref, akv_ref, akr_ref, qnw_ref, kvnw_ref, wq_ref, wkn_ref, wv_ref,
                     tqc_ref, tqs_ref, tkc_ref, tks_ref, q_out, k_out, v_out):
    qn = _rms(aq_ref[...].astype(F32), qnw_ref[...]).astype(BF16)
    qq = _dot(qn, wq_ref[...])
    tqc = tqc_ref[...]
    tqs = tqs_ref[...]
    for h in range(MLA_HEADS):
        blk = qq[:, h * LANE:(h + 1) * LANE]
        swapped = pltpu.roll(blk, LANE - MLA_ROPE, axis=1)
        q_out[:, h * LANE:(h + 1) * LANE] = (blk * tqc + swapped * tqs).astype(BF16)
    kvn = _rms(akv_ref[...].astype(F32), kvnw_ref[...]).astype(BF16)
    kn = _dot(kvn, wkn_ref[...])
    r = akr_ref[...].astype(F32)
    kr = r[:, :LANE] * tkc_ref[...] + r[:, LANE:] * tks_ref[...]
    for h in range(MLA_HEADS):
        lo = h * LANE
        k_out[:, lo:lo + LANE] = (kn[:, lo:lo + LANE] + kr).astype(BF16)
    v_out[...] = _dot(kvn, wv_ref[...]).astype(BF16)


def _mla_prep(z, qnw, kvnw, wq, wkn, wv, tabs, layer, n_lat_tiles):
    ntok = z.shape[0]
    tm = TM_IN
    per_batch = SEQ // tm
    tab_idx = lambda i: (jnp.where(i < n_lat_tiles, i % per_batch, per_batch), 0)
    hw = MLA_HEADS * LANE
    tab_spec = pl.BlockSpec((tm, LANE), tab_idx)
    return pl.pallas_call(
        _mla_prep_kernel,
        name="mla_prep",
        grid=(ntok // tm,),
        in_specs=[
            pl.BlockSpec((tm, MLA_Q_RANK), lambda i: (i, Z_AQ // MLA_Q_RANK)),
            pl.BlockSpec((tm, MLA_KV_RANK), lambda i: (i, Z_AKV // MLA_KV_RANK)),
            pl.BlockSpec((tm, 2 * LANE), lambda i: (i, Z_AKR // (2 * LANE))),
            _layer_spec(qnw, layer), _layer_spec(kvnw, layer),
            _layer_spec(wq, layer), _layer_spec(wkn, layer), _layer_spec(wv, layer),
            tab_spec, tab_spec, tab_spec, tab_spec,
        ],
        out_specs=[
            pl.BlockSpec((tm, hw), lambda i: (i, 0)),
            pl.BlockSpec((tm, hw), lambda i: (i, 0)),
            pl.BlockSpec((tm, MLA_HEADS * MLA_V), lambda i: (i, 0)),
        ],
        out_shape=[
            jax.ShapeDtypeStruct((ntok, hw), BF16),
            jax.ShapeDtypeStruct((ntok, hw), BF16),
            jax.ShapeDtypeStruct((ntok, MLA_HEADS * MLA_V), BF16),
        ],
        compiler_params=_cparams(("parallel",)),
    )(z, z, z, qnw, kvnw, wq, wkn, wv, *tabs)


def _attend_t(qs, n_chunks, chunk):
    if len(qs) > HEAD_GROUP:
        outs = []
        for g in range(0, len(qs), HEAD_GROUP):
            outs += _attend_t(qs[g:g + HEAD_GROUP], n_chunks, lambda h, c, g=g: chunk(g + h, c))
        return outs
    heads = range(len(qs))
    dv = chunk(0, 0)[1].shape[0]
    m = o = None

    def scores(c):
        parts = [chunk(h, c) for h in heads]
        s = [_dot_nt(parts[h][0], qs[h]) for h in heads]
        return parts, [s[h] if parts[h][2] is None else s[h] + parts[h][2] for h in heads]

    ahead =[scores(c) for c in range(min(SCORE_LOOKAHEAD, n_chunks))]
    for c in range(n_chunks):
        parts, s = ahead.pop(0)
        if c + SCORE_LOOKAHEAD < n_chunks:
            ahead.append(scores(c + SCORE_LOOKAHEAD))
        mc = [jnp.max(s[h], axis=0, keepdims=True) for h in heads]
        ones = jnp.ones((BF16_SUBLANES, parts[0][1].shape[1]), BF16)
        vt1 = [jnp.concatenate([parts[h][1], ones], axis=0) for h in heads]
        if m is None:
            m = mc
            o = [_dot(vt1[h], jnp.exp2(s[h] - m[h]).astype(BF16)) for h in heads]
        else:
            m_new = [jnp.maximum(m[h], mc[h]) for h in heads]
            alpha = [jnp.exp2(m[h] - m_new[h]) for h in heads]
            o = [o[h] * alpha[h] + _dot(vt1[h], jnp.exp2(s[h] - m_new[h]).astype(BF16)) for h in heads]
            m = m_new
    return [o[h][:dv] / o[h][dv:dv + 1] for h in heads]


def _transpose_rows(src_ref, dst_ref, rows):
    for r in range(0, rows, TQ):
        dst_ref[:, r:r + TQ] = src_ref[r:r + TQ, :].astype(F32).T.astype(BF16)


def _mla_attn_kernel(q_ref, kl_ref, kc_ref, vl_ref, vc_ref, o_ref, vlt_s, vct_s, *, n_lat_q):
    t = pl.program_id(1)

    @pl.when(t == 0)
    def _():
        _transpose_rows(vl_ref, vlt_s, SEQ)
        _transpose_rows(vc_ref, vct_s, CTX_LEN)

    def attend(with_lat):
        def chunk(h, c):
            cols = slice(h * LANE, (h + 1) * LANE)
            rows = slice(h * MLA_V, (h + 1) * MLA_V)
            if c == 0:
                return kc_ref[:, cols], vct_s[rows, :], None
            keys = slice((c - 1) * MLA_KEY_CHUNK, c * MLA_KEY_CHUNK)
            return kl_ref[keys, cols], vlt_s[rows, keys], None

        qs = [q_ref[:, h * LANE:(h + 1) * LANE] for h in range(MLA_HEADS)]
        outs = _attend_t(qs, 1 + (SEQ // MLA_KEY_CHUNK if with_lat else 0), chunk)
        o_ref[...] = jnp.concatenate(outs, axis=0).T.astype(BF16)

    pl.when(t < n_lat_q)(lambda: attend(True))
    pl.when(t >= n_lat_q)(lambda: attend(False))


def _q_row_block(n_lat_q, ctx_base):
    return lambda b, t: jnp.where(t < n_lat_q, b * n_lat_q + t, ctx_base + b)


def _mla_attn(q, k, v, batch, need_ctx):
    ntok = q.shape[0]
    n_lat_q = SEQ // TQ
    ctx_base = batch * SEQ // CTX_LEN
    qrow = _q_row_block(n_lat_q, ctx_base)
    hw = MLA_HEADS * LANE
    vw = MLA_HEADS * MLA_V
    return pl.pallas_call(
        functools.partial(_mla_attn_kernel, n_lat_q=n_lat_q),
        name="mla_attn",
        grid=(batch, n_lat_q + (1 if need_ctx else 0)),
        in_specs=[
            pl.BlockSpec((TQ, hw), lambda b, t: (qrow(b, t), 0)),
            pl.BlockSpec((SEQ, hw), lambda b, t: (b, 0)),
            pl.BlockSpec((CTX_LEN, hw), lambda b, t: (ctx_base + b, 0)),
            pl.BlockSpec((SEQ, vw), lambda b, t: (b, 0)),
            pl.BlockSpec((CTX_LEN, vw), lambda b, t: (ctx_base + b, 0)),
        ],
        out_specs=pl.BlockSpec((TQ, vw), lambda b, t: (qrow(b, t), 0)),
        out_shape=jax.ShapeDtypeStruct((ntok, vw), BF16),
        scratch_shapes=[pltpu.VMEM((vw, SEQ), BF16), pltpu.VMEM((vw, CTX_LEN), BF16)],
        compiler_params=_cparams(("parallel", "arbitrary")),
    )(q, k, k, v, v)


def _gla_kernel(ql_ref, qc_ref, kl_ref, kc_ref, vl_ref, vc_ref, gl_ref, gc_ref, lrl_ref, lrc_ref,
                wg_ref, bg_ref, nw_ref, yl_ref, yc_ref,
                v_s, qd_s, ke_s, dec_s, o_s, st_s):
    n_tok = SEQ + CTX_LEN
    n_chunk = n_tok // GLA_CHUNK
    ctx_chunks = CTX_LEN // GLA_CHUNK
    ck = GLA_CHUNK
    tile = GLA_TILE
    per_tile = tile // ck
    qk_w = GLA_QK_W
    pair_k = GLA_PAIR * GLA_DK
    pair_v = GLA_PAIR * GLA_DV
    n_pairs = GLA_HEADS // GLA_PAIR

    v_s[0:SEQ, :] = vl_ref[...]
    v_s[SEQ:n_tok, :] = vc_ref[...]
    st_s[...] = jnp.zeros_like(st_s)

    row = lax.broadcasted_iota(jnp.int32, (tile, tile), 0)
    col = lax.broadcasted_iota(jnp.int32, (tile, tile), 1)
    same_chunk = (row // ck) == (col // ck)
    keep = (same_chunk & (col <= row), same_chunk & (col >= row))
    tri3 = tuple(jnp.concatenate([jnp.where(kp, 1.0, 0.0).astype(BF16)] * 3, axis=1) for kp in keep)
    q_head = lax.broadcasted_iota(jnp.int32, (tile, qk_w), 1) // GLA_DK
    bd_pair = (lax.broadcasted_iota(jnp.int32, (pair_k, pair_v), 0) // GLA_DK
               == lax.broadcasted_iota(jnp.int32, (pair_k, pair_v), 1) // GLA_DV)

    def log_decay(lr):
        zz = _dot(lr, wg_ref[...]) + bg_ref[...]
        return (jnp.minimum(zz, 0.0) - jnp.log1p(jnp.exp(-jnp.abs(zz)))) * (1.0 / GLA_GATE_TAU)

    def prologue(tiles):
        nt = range(len(tiles))
        dirs = range(2)
        heads = range(GLA_HEADS)
        rows = [pl.ds(tiles[t][4], tile) for t in nt]
        q = [tiles[t][0].astype(F32) * (GLA_DK ** -0.5) for t in nt]
        k = [tiles[t][1].astype(F32) for t in nt]
        la = [log_decay(tiles[t][3]) for t in nt]
        cum = [[_dot(tri3[d], jnp.concatenate(_split3(la[t][:, d * qk_w:(d + 1) * qk_w]), axis=0))
                for d in dirs] for t in nt]
        edge = (ck - 1, 0)
        lasts = [[[cum[t][d][c * ck + edge[d]:c * ck + edge[d] + 1, :] for c in range(per_tile)]
                  for d in dirs] for t in nt]
        last = [[jnp.concatenate([jnp.broadcast_to(x, (ck, qk_w)) for x in lasts[t][d]], axis=0)
                 for d in dirs] for t in nt]
        q_dec = [[(q[t] * jnp.exp(cum[t][d])).astype(BF16) for d in dirs] for t in nt]
        k_inv = [[(k[t] * jnp.exp(-cum[t][d])).astype(BF16) for d in dirs] for t in nt]
        for t in nt:
            for d in dirs:
                qd_s[d, rows[t], :] = q_dec[t][d]
                ke_s[d, rows[t], :] = (k[t] * jnp.exp(last[t][d] - cum[t][d])).astype(BF16)
                for c in range(per_tile):
                    dec_s[d, tiles[t][4] // ck + c] = jnp.broadcast_to(jnp.exp(lasts[t][d][c]), (8, qk_w))
        qm = [[[jnp.where(q_head == h, q_dec[t][d], jnp.zeros((), BF16)) for h in heads] for d in dirs] for t in nt]
        s = [[[_dot_nt(qm[t][d][h], k_inv[t][d]) for h in heads] for d in dirs] for t in nt]
        att = [[[jnp.where(keep[d], s[t][d][h], 0.0).astype(BF16) for h in heads] for d in dirs] for t in nt]
        o = [[[_dot(att[t][d][h], tiles[t][2][:, h * GLA_DV:(h + 1) * GLA_DV]) for h in heads]
              for d in dirs] for t in nt]
        for t in nt:
            for d in dirs:
                for h in heads:
                    o_s[d, rows[t], h * GLA_DV:(h + 1) * GLA_DV] = o[t][d][h]

    def lat_tiles(i, carry):
        tiles = []
        for j in range(GLA_TILES_PER_STEP):
            r0 = pl.multiple_of((i * GLA_TILES_PER_STEP + j) * tile, tile)
            rows = pl.ds(r0, tile)
            tiles.append((ql_ref[rows, :], kl_ref[rows, :], vl_ref[rows, :], lrl_ref[rows, :], r0))
        prologue(tiles)
        return carry

    lax.fori_loop(0, SEQ // (tile * GLA_TILES_PER_STEP), lat_tiles, 0)
    for r0 in range(0, CTX_LEN, tile):
        rows = slice(r0, r0 + tile)
        prologue([(qc_ref[rows, :], kc_ref[rows, :], vc_ref[rows, :], lrc_ref[rows, :], SEQ + r0)])

    def scan_step(n, d):
        rows = pl.ds(pl.multiple_of(n * ck, ck), ck)
        dec = dec_s[d, n]
        for p in range(n_pairs):
            kcols = slice(p * pair_k, (p + 1) * pair_k)
            vcols = slice(p * pair_v, (p + 1) * pair_v)
            st = st_s[d, p]
            o_s[d, rows, vcols] = o_s[d, rows, vcols] + _dot(qd_s[d, rows, kcols], st.astype(BF16))
            kv = _dot_tn(ke_s[d, rows, kcols], v_s[rows, vcols])
            decay_rows = jnp.broadcast_to(dec[0:1, kcols], (pair_v, pair_k)).T
            st_s[d, p] = st * decay_rows + jnp.where(bd_pair, kv, 0.0)

    def body(i, carry):
        nf = jnp.where(i < ctx_chunks, n_chunk - ctx_chunks + i, i - ctx_chunks)
        nb = n_chunk - 1 - i
        scan_step(nf, 0)
        scan_step(nb, 1)
        return carry

    lax.fori_loop(0, n_chunk, body, 0, unroll=2)

    nw = nw_ref[...]

    def finish(r0, rows, g):
        o = o_s[0, r0:r0 + rows, :] + o_s[1, r0:r0 + rows, :]
        g = g.astype(F32)
        parts = []
        for h in range(GLA_HEADS):
            parts.append(_rms(o[:, h * GLA_DV:(h + 1) * GLA_DV], nw))
        y = jnp.concatenate(parts, axis=1)
        return (y * (g * _sigmoid(g))).astype(BF16)

    rt = GLA_TILE
    for i in range(SEQ // rt):
        yl_ref[i * rt:(i + 1) * rt, :] = finish(i * rt, rt, gl_ref[i * rt:(i + 1) * rt, :])
    yc_ref[...] = finish(SEQ, CTX_LEN, gc_ref[...])


def _gla(z, wg, bg, nw, layer, batch):
    ctx_base = batch * SEQ // CTX_LEN
    n_tok = SEQ + CTX_LEN

    def lat(width, col0):
        return pl.BlockSpec((SEQ, width), lambda b: (b, col0 // width))

    def ctx(width, col0):
        return pl.BlockSpec((CTX_LEN, width), lambda b: (ctx_base + b, col0 // width))

    return pl.pallas_call(
        _gla_kernel,
        name="gla_scan",
        grid=(batch,),
        in_specs=[
            lat(GLA_QK_W, Z_BQ), ctx(GLA_QK_W, Z_BQ),
            lat(GLA_QK_W, Z_BK), ctx(GLA_QK_W, Z_BK),
            lat(GLA_V_W, Z_BV), ctx(GLA_V_W, Z_BV),
            lat(GLA_V_W, Z_BG), ctx(GLA_V_W, Z_BG),
            lat(LANE, Z_LR), ctx(LANE, Z_LR),
            _layer_spec(wg, layer), _layer_spec(bg, layer), _layer_spec(nw, layer),
        ],
        out_specs=[
            pl.BlockSpec((SEQ, GLA_V_W), lambda b: (b, 0)),
            pl.BlockSpec((CTX_LEN, GLA_V_W), lambda b: (b, 0)),
        ],
        out_shape=[
            jax.ShapeDtypeStruct((batch * SEQ, GLA_V_W), BF16),
            jax.ShapeDtypeStruct((batch * CTX_LEN, GLA_V_W), BF16),
        ],
        scratch_shapes=[
            pltpu.VMEM((n_tok, GLA_V_W), BF16),
            pltpu.VMEM((2, n_tok, GLA_QK_W), BF16),
            pltpu.VMEM((2, n_tok, GLA_QK_W), BF16),
            pltpu.VMEM((2, n_tok // GLA_CHUNK, 8, GLA_QK_W), F32),
            pltpu.VMEM((2, n_tok, GLA_V_W), F32),
            pltpu.VMEM((2, GLA_HEADS // GLA_PAIR, GLA_PAIR * GLA_DK, GLA_PAIR * GLA_DV), F32),
        ],
        compiler_params=_cparams(("parallel",)),
    )(z, z, z, z, z, z, z, z, z, z, wg, bg, nw)


def _na_kernel(q_ref, kl_ref, kc_ref, vl_ref, vc_ref, bias_ref, o_ref, vlt_s, vct_s, *, n_lat_q):
    lane = lax.broadcasted_iota(jnp.int32, (TQ, LANE), 1)
    t = pl.program_id(1)

    @pl.when(t == 0)
    def _():
        _transpose_rows(vl_ref, vlt_s, SEQ)
        _transpose_rows(vc_ref, vct_s, CTX_LEN)

    def attend(with_band):
        if with_band:
            r0 = jnp.clip(NA_QROWS * t - NA_KH // 2, 0, GRID_ROWS - NA_BAND_ROWS)
            start = pl.multiple_of(r0 * GRID_W, NA_QBLK)
            kind = jnp.where(t == 0, 0, jnp.where(t == n_lat_q - 1, 2, 1))

        def chunk(h, c):
            cols = slice((h // 2) * LANE, (h // 2 + 1) * LANE)
            rows = slice(h * NA_HEAD_DIM, (h + 1) * NA_HEAD_DIM)
            if c == 0:
                return kc_ref[:, cols], vct_s[rows, :], None
            off = (c - 1) * KEY_CHUNK
            keys = pl.ds(pl.multiple_of(start + off, KEY_CHUNK), KEY_CHUNK)
            bias_rows = pl.ds(pl.multiple_of(kind * NA_BAND + off, KEY_CHUNK), KEY_CHUNK)
            return kl_ref[keys, cols], vlt_s[rows, keys], bias_ref[h, bias_rows, :]

        qs = []
        for h in range(NA_HEADS):
            head_lanes = lane < NA_HEAD_DIM if h % 2 == 0 else lane >= NA_HEAD_DIM
            qs.append(jnp.where(head_lanes, q_ref[:, (h // 2) * LANE:(h // 2 + 1) * LANE], jnp.zeros((), BF16)))
        outs = _attend_t(qs, 1 + (NA_BAND // KEY_CHUNK if with_band else 0), chunk)
        o_ref[...] = jnp.concatenate(outs, axis=0).T.astype(BF16)

    pl.when(t < n_lat_q)(lambda: attend(True))
    pl.when(t >= n_lat_q)(lambda: attend(False))


def _na(z, bias, layer, batch, need_ctx):
    ntok = z.shape[0]
    n_lat_q = SEQ // TQ
    ctx_base = batch * SEQ // CTX_LEN
    qrow = _q_row_block(n_lat_q, ctx_base)
    w = NA_W
    return pl.pallas_call(
        functools.partial(_na_kernel, n_lat_q=n_lat_q),
        name="na_attn",
        grid=(batch, n_lat_q + (1 if need_ctx else 0)),
        in_specs=[
            pl.BlockSpec((TQ, w), lambda b, t: (qrow(b, t), Z_CQ // w)),
            pl.BlockSpec((SEQ, w), lambda b, t: (b, Z_CK // w)),
            pl.BlockSpec((CTX_LEN, w), lambda b, t: (ctx_base + b, Z_CK // w)),
            pl.BlockSpec((SEQ, w), lambda b, t: (b, Z_CV // w)),
            pl.BlockSpec((CTX_LEN, w), lambda b, t: (ctx_base + b, Z_CV // w)),
            _layer_spec(bias, layer, pipeline_mode=pl.Buffered(1)),
        ],
        out_specs=pl.BlockSpec((TQ, w), lambda b, t: (qrow(b, t), 0)),
        out_shape=jax.ShapeDtypeStruct((ntok, w), BF16),
        scratch_shapes=[pltpu.VMEM((w, SEQ), BF16), pltpu.VMEM((w, CTX_LEN), BF16)],
        compiler_params=_cparams(("parallel", "arbitrary")),
    )(z, z, z, z, z, bias)


def _merge_kernel(hl_ref, hc_ref, ya_ref, ybl_ref, ybc_ref, yc_ref, ga_ref, gb_ref, gc_ref, mod_ref,
                  wa_ref, wb_ref, wc_ref, wo_ref, o_ref, *, n_lat_tiles):
    is_lat = pl.program_id(0) < n_lat_tiles
    h = jnp.where(is_lat, hl_ref[...], hc_ref[...])
    yb = jnp.where(is_lat, ybl_ref[...], ybc_ref[...])
    m = (_sigmoid(ga_ref[...].astype(F32)) * _dot(ya_ref[...], wa_ref[...])
         + _sigmoid(gb_ref[...].astype(F32)) * _dot(yb, wb_ref[...])
         + _sigmoid(gc_ref[...].astype(F32)) * _dot(yc_ref[...], wc_ref[...]))
    o_ref[...] = h + mod_ref[0, 2:3, :] * _dot(m.astype(BF16), wo_ref[...])


def _merge(h_lat, h_ctx, ctx_block0, ya, yb_lat, yb_ctx, yc, z, mod_all, wa, wb, wc, wo, layer, batch, need_ctx):
    d = h_lat.shape[1]
    ntok = batch * (SEQ + CTX_LEN)
    tm = TM_TOK
    n_lat_tiles = batch * SEQ // tm
    n_tiles = ntok // tm if need_ctx else n_lat_tiles
    per_batch = SEQ // tm
    bw = wa.shape[1]
    tok = lambda width: pl.BlockSpec((tm, width), lambda i: (i, 0))
    gate = lambda g: pl.BlockSpec((tm, d), lambda i: (i, Z_GATES // d + g))
    return pl.pallas_call(
        functools.partial(_merge_kernel, n_lat_tiles=n_lat_tiles),
        name="merge",
        grid=(n_tiles,),
        in_specs=[
            *_h_specs(tm, d, n_lat_tiles, batch * CTX_LEN // tm, ctx_block0), tok(bw),
            pl.BlockSpec((tm, bw), lambda i: (jnp.minimum(i, n_lat_tiles - 1), 0)),
            pl.BlockSpec((tm, bw), lambda i: (jnp.maximum(i - n_lat_tiles, 0), 0)),
            tok(bw), gate(0), gate(1), gate(2),
            _mod_spec(mod_all, layer, lambda i: jnp.where(i < n_lat_tiles, i // per_batch, batch)),
            _layer_spec(wa, layer), _layer_spec(wb, layer), _layer_spec(wc, layer), _layer_spec(wo, layer),
        ],
        out_specs=tok(d),
        out_shape=jax.ShapeDtypeStruct((ntok, d), F32),
        compiler_params=_cparams(("parallel",)),
    )(h_lat, h_ctx, ya, yb_lat, yb_ctx, yc, z, z, z, mod_all, wa, wb, wc, wo)


def _ffn_kernel(h_ref, mod_ref, nw_ref, win_ref, wout_ref, fw_ref, o_ref, *, final):
    h = h_ref[...]
    u = (_rms(h, nw_ref[...]) * (1.0 + mod_ref[0, 4:5, :]) + mod_ref[0, 3:4, :]).astype(BF16)
    acc = None
    for lo, hi in FFN_CHUNKS:
        g = _dot(u, win_ref[:, lo:hi])
        up = _dot(u, win_ref[:, FFN_HIDDEN + lo:FFN_HIDDEN + hi])
        act = (g * _sigmoid(g) * up).astype(BF16)
        part = _dot(act, wout_ref[lo:hi, :])
        acc = part if acc is None else acc + part
    out = h + mod_ref[0, 5:6, :] * acc
    if final:
        out = _rms(out, fw_ref[...])
    o_ref[...] = out


def _ffn(h, mod_all, norm_w, w_in, w_out, final_w, layer, batch, need_ctx, final):
    ntok, d = h.shape
    tm = TM_TOK
    n_lat_tiles = batch * SEQ // tm
    n_tiles = ntok // tm if need_ctx else n_lat_tiles
    per_batch = SEQ // tm
    resident = dict(pipeline_mode=pl.Buffered(1))
    out_rows = n_tiles * tm if final else ntok
    return pl.pallas_call(
        functools.partial(_ffn_kernel, final=final),
        name="ffn",
        grid=(n_tiles,),
        in_specs=[
            pl.BlockSpec((tm, d), lambda i: (i, 0)),
            _mod_spec(mod_all, layer, lambda i: jnp.where(i < n_lat_tiles, i // per_batch, batch)),
            _layer_spec(norm_w, layer),
            _layer_spec(w_in, layer, **resident),
            _layer_spec(w_out, layer, **resident),
            pl.BlockSpec((1, d), lambda i: (0, 0)),
        ],
        out_specs=pl.BlockSpec((tm, d), lambda i: (i, 0)),
        out_shape=jax.ShapeDtypeStruct((out_rows, d), F32),
        compiler_params=_cparams(("parallel",)),
    )(h, mod_all, norm_w, w_in, w_out, final_w)


def _pad_cols(w, width):
    return jnp.pad(w, ((0, 0), (0, 0), (0, width - w.shape[-1])))


def _prep_w_in(w_in):
    a0 = 0
    b0 = MLA_Q_RANK + MLA_KV_RANK + MLA_ROPE
    c0 = b0 + 2 * GLA_QK_W + 2 * GLA_V_W + 2 * GLA_GATE_RANK
    g0 = c0 + 3 * NA_W
    sl = lambda lo, n: w_in[:, :, lo:lo + n]
    aq = sl(a0, MLA_Q_RANK)
    akv = sl(a0 + MLA_Q_RANK, MLA_KV_RANK)
    kr0 = a0 + MLA_Q_RANK + MLA_KV_RANK
    half = MLA_ROPE // 2
    kr = sl(kr0, MLA_ROPE)
    kr_sw = jnp.concatenate([sl(kr0 + half, half), sl(kr0, half)], axis=-1)
    zeros = lambda n: jnp.zeros(w_in.shape[:2] + (n,), w_in.dtype)
    akr = jnp.concatenate([zeros(MLA_NOPE), kr, zeros(LANE - MLA_NOPE - MLA_ROPE),
                           zeros(MLA_NOPE), kr_sw, zeros(LANE - MLA_NOPE - MLA_ROPE)], axis=-1)
    bq = sl(b0, GLA_QK_W)
    bk = sl(b0 + GLA_QK_W, GLA_QK_W)
    bv = sl(b0 + 2 * GLA_QK_W, GLA_V_W)
    bg = sl(b0 + 2 * GLA_QK_W + GLA_V_W, GLA_V_W)
    lr = _pad_cols(sl(b0 + 2 * GLA_QK_W + 2 * GLA_V_W, 2 * GLA_GATE_RANK), LANE)
    cq = sl(c0, NA_W) * (NA_SCALE * LOG2E)
    ck = sl(c0 + NA_W, NA_W)
    cv = sl(c0 + 2 * NA_W, NA_W)
    gates = sl(g0, 3 * D_MODEL)
    pieces = [aq, lr, cq, ck, cv, bv, bg, bq, bk, akv, akr, gates]
    out = jnp.concatenate([p.astype(BF16) for p in pieces], axis=-1)
    assert out.shape[-1] == Z_WIDTH
    return out


def _prep_mla_weights(w_q_up, w_kv_up):
    depth = w_q_up.shape[0]
    hd = MLA_NOPE + MLA_ROPE
    half = MLA_ROPE // 2
    wq = w_q_up.reshape(depth, MLA_Q_RANK, MLA_HEADS, hd)
    assert LANE - hd == MLA_ROPE
    wq_p = jnp.concatenate([wq, wq[..., MLA_NOPE + half:], wq[..., MLA_NOPE:MLA_NOPE + half]],
                           axis=-1).reshape(depth, MLA_Q_RANK, -1)
    wkv = w_kv_up.reshape(depth, MLA_KV_RANK, MLA_HEADS, MLA_NOPE + MLA_V)
    wkn = jnp.concatenate([wkv[..., :MLA_NOPE],
                           jnp.zeros((depth, MLA_KV_RANK, MLA_HEADS, LANE - MLA_NOPE), w_kv_up.dtype)], axis=-1)
    wv = wkv[..., MLA_NOPE:]
    return (wq_p.astype(BF16), wkn.reshape(depth, MLA_KV_RANK, -1).astype(BF16),
            wv.reshape(depth, MLA_KV_RANK, -1).astype(BF16))


def _rope_tables():
    t = np.arange(SEQ)
    rows = (t // GRID_W).astype(np.float32)
    cols = (t % GRID_W).astype(np.float32)
    n_freq = MLA_ROPE // 4
    inv_freq = jnp.asarray(ROPE_BASE, F32) ** (-jnp.arange(n_freq, dtype=F32) / n_freq)
    ang = jnp.concatenate([jnp.asarray(rows)[:, None] * inv_freq, jnp.asarray(cols)[:, None] * inv_freq], axis=-1)
    cos, sin = jnp.cos(ang), jnp.sin(ang)
    cos = jnp.concatenate([cos, jnp.ones((TM_IN, MLA_ROPE // 2), F32)], axis=0)
    sin = jnp.concatenate([sin, jnp.zeros((TM_IN, MLA_ROPE // 2), F32)], axis=0)
    n = cos.shape[0]
    tail = jnp.zeros((n, LANE - MLA_NOPE - MLA_ROPE), F32)
    c_rot = jnp.concatenate([cos, cos], axis=-1)
    s_rot = jnp.concatenate([-sin, sin], axis=-1)
    nope1 = jnp.ones((n, MLA_NOPE), F32)
    nope0 = jnp.zeros((n, MLA_NOPE), F32)
    tqc = jnp.concatenate([nope1, c_rot, tail], axis=-1) * (MLA_SCALE * LOG2E)
    tqs = jnp.concatenate([nope0, s_rot, tail], axis=-1) * (MLA_SCALE * LOG2E)
    tkc = jnp.concatenate([nope0, c_rot, tail], axis=-1)
    tks = jnp.concatenate([nope0, s_rot, tail], axis=-1)
    return tqc, tqs, tkc, tks


def _na_bias_tables(rpb):
    depth = rpb.shape[0]
    w2 = 2 * GRID_W
    lo = GRID_W - NA_KW
    u = jnp.pad(rpb.astype(F32), ((0, 0), (0, 0), (0, 0), (lo, w2 - lo - (2 * NA_KW - 1))))
    toep = jnp.stack([u[..., GRID_W - 1 - qc:w2 - 1 - qc] for qc in range(GRID_W)], axis=-2)
    col = np.arange(GRID_W)
    cs = np.clip(col - NA_KW // 2, 0, GRID_W - NA_KW)
    col_ok = (col[None, :] >= cs[:, None]) & (col[None, :] < cs[:, None] + NA_KW)
    toep = jnp.swapaxes(jnp.where(jnp.asarray(col_ok), toep * LOG2E, -1e30), -1, -2)
    masked = jnp.full((depth, NA_HEADS, GRID_W, GRID_W), -1e30, F32)
    n_blocks = GRID_ROWS // NA_QROWS
    k_rows = []
    for g in (0, 1, n_blocks - 1):
        band0 = int(np.clip(NA_QROWS * g - NA_KH // 2, 0, GRID_ROWS - NA_BAND_ROWS))
        for i in range(NA_BAND_ROWS):
            kr = band0 + i
            blocks = []
            for j in range(NA_QROWS):
                qr = NA_QROWS * g + j
                r0 = int(np.clip(qr - NA_KH // 2, 0, GRID_ROWS - NA_KH))
                blocks.append(toep[:, :, kr - qr + NA_KH - 1] if r0 <= kr < r0 + NA_KH else masked)
            k_rows.append(jnp.concatenate(blocks, axis=-1))
    return jnp.concatenate(k_rows, axis=-2)


def _prep_gla_gate(w_f, b_f, w_b, b_b):
    depth = w_f.shape[0]
    wg = jnp.zeros((depth, LANE, 2 * GLA_QK_W), F32)
    wg = wg.at[:, :GLA_GATE_RANK, :GLA_QK_W].set(w_f)
    wg = wg.at[:, GLA_GATE_RANK:2 * GLA_GATE_RANK, GLA_QK_W:].set(w_b)
    bg = jnp.concatenate([b_f, b_b], axis=-1)[:, None, :]
    return wg.astype(BF16), bg


def kernel(x, c, ctx, c_ctx, w_mod, b_mod, norm1_w, w_in, mla_q_norm_w, mla_kv_norm_w, mla_w_q_up, mla_w_kv_up, gla_w_gate_f, gla_b_gate_f, gla_w_gate_b, gla_b_gate_b, gla_norm_w, na_rpb, w_a_o, w_b_o, w_c_o, w_out, norm2_w, w_ffn_in, w_ffn_out, final_norm_w):
    batch, seq, d = x.shape
    depth = w_mod.shape[0]
    assert seq == SEQ and d == D_MODEL and ctx.shape[1] == CTX_LEN
    assert (batch * CTX_LEN) % TM_IN == 0
    n_lat_in = batch * SEQ // TM_IN

    mod_rows = -(-(batch + 1) // 8) * 8
    c_all = jnp.concatenate([c, c_ctx[None, :], jnp.zeros((mod_rows - batch - 1, d), F32)], axis=0)
    mod_all = _modulation(c_all, w_mod, b_mod).reshape(depth, mod_rows, 6, d)

    w_in_p = _prep_w_in(w_in)
    wq_p, wkn_p, wv_p = _prep_mla_weights(mla_w_q_up, mla_w_kv_up)
    tabs = _rope_tables()
    na_bias = _na_bias_tables(na_rpb)
    wg_p, bg_p = _prep_gla_gate(gla_w_gate_f, gla_b_gate_f, gla_w_gate_b, gla_b_gate_b)
    wa, wb, wc, wo = (w.astype(BF16) for w in (w_a_o, w_b_o, w_c_o, w_out))
    wfi, wfo = w_ffn_in.astype(BF16), w_ffn_out.astype(BF16)
    n1, n2, qnw, kvnw, gnw = (w[:, None, :] for w in (norm1_w, norm2_w, mla_q_norm_w, mla_kv_norm_w, gla_norm_w))

    hs = (x.reshape(batch * SEQ, d), ctx.reshape(batch * CTX_LEN, d), 0)
    for i in range(depth):
        need_ctx = i < depth - 1
        z = _inproj(*hs, mod_all, n1, w_in_p, i, batch)
        q, k, v = _mla_prep(z, qnw, kvnw, wq_p, wkn_p, wv_p, tabs, i, n_lat_in)
        ya = _mla_attn(q, k, v, batch, need_ctx)
        yb_lat, yb_ctx = _gla(z, wg_p, bg_p, gnw, i, batch)
        yc = _na(z, na_bias, i, batch, need_ctx)
        h = _merge(*hs, ya, yb_lat, yb_ctx, yc, z, mod_all, wa, wb, wc, wo, i, batch, need_ctx)
        h = _ffn(h, mod_all, n2, wfi, wfo, final_norm_w[None, :], i, batch, need_ctx, not need_ctx)
        hs = (h, h, batch * SEQ // TM_TOK)
    return h.reshape(batch, SEQ, d)
```

```python
import functools

import jax
import jax.numpy as jnp
import numpy as np
from jax import lax
from jax.experimental import pallas as pl
from jax.experimental.pallas import tpu as pltpu

F32 = jnp.float32
BF16 = jnp.bfloat16

D_MODEL = 1024
SEQ = 2048
CTX_LEN = 256
GRID_W = 64
GRID_ROWS = SEQ // GRID_W
EPS = 1e-6
ROPE_BASE = 10000.0

MLA_HEADS = 8
MLA_Q_RANK = 384
MLA_KV_RANK = 256
MLA_NOPE = 64
MLA_ROPE = 32
MLA_V = 64
MLA_SCALE = (MLA_NOPE + MLA_ROPE) ** -0.5

GLA_HEADS = 4
GLA_DK = 64
GLA_DV = 128
GLA_GATE_RANK = 16
GLA_GATE_TAU = 16.0
GLA_CHUNK = 64
GLA_TILE = 256
GLA_TILES_PER_STEP = 2
GLA_PAIR = 2
GLA_QK_W = GLA_HEADS * GLA_DK
GLA_V_W = GLA_HEADS * GLA_DV

NA_HEADS = 8
NA_HEAD_DIM = 64
NA_KH = 8
NA_KW = 16
NA_W = NA_HEADS * NA_HEAD_DIM
NA_SCALE = NA_HEAD_DIM ** -0.5
LOG2E = 1.4426950408889634
NA_QROWS = 4
NA_BAND_ROWS = NA_KH + NA_QROWS
NA_BAND = NA_BAND_ROWS * GRID_W
NA_QBLK = NA_QROWS * GRID_W

FFN_HIDDEN = 2816
MXU_DIM = 256
_FFN_SPLIT = (FFN_HIDDEN // MXU_DIM + 1) // 2 * MXU_DIM
FFN_CHUNKS = ((0, _FFN_SPLIT), (_FFN_SPLIT, FFN_HIDDEN))
LANE = 128
BF16_SUBLANES = 16

Z_AQ = 0
Z_LR = 384
Z_CQ = 512
Z_CK = 1024
Z_CV = 1536
Z_BV = 2048
Z_BG = 2560
Z_BQ = 3072
Z_BK = 3328
Z_AKV = 3584
Z_AKR = 3840
Z_GATES = 4096
Z_WIDTH = 7168

TM_IN = 1024
TN_IN = 1024
TM_TOK = 512
TQ = 256
KEY_CHUNK = 256
MLA_KEY_CHUNK = 256
SCORE_LOOKAHEAD = 2
KEY_STREAMS = 2
HEAD_GROUP = 8
VMEM_LIMIT = 56 * 1024 * 1024


def _cparams(sem):
    return pltpu.CompilerParams(dimension_semantics=sem, vmem_limit_bytes=VMEM_LIMIT)


def _dot(a, b):
    return jnp.dot(a, b, preferred_element_type=F32)


def _dot_nt(a, b):
    return lax.dot_general(a, b, (((1,), (1,)), ((), ())), preferred_element_type=F32)


def _dot_tn(a, b):
    return lax.dot_general(a, b, (((0,), (0,)), ((), ())), preferred_element_type=F32)


def _sigmoid(x):
    return 0.5 * jnp.tanh(0.5 * x) + 0.5


def _rms(x, w):
    return x * lax.rsqrt(jnp.mean(x * x, axis=-1, keepdims=True) + EPS) * w


def _split3(x):
    hi = x.astype(BF16)
    r1 = x - hi.astype(F32)
    mid = r1.astype(BF16)
    lo = (r1 - mid.astype(F32)).astype(BF16)
    return hi, mid, lo


def _mod_kernel(c_ref, w_ref, b_ref, o_ref):
    c = c_ref[...]
    a = c * _sigmoid(c)
    a_hi = a.astype(BF16)
    a_lo = (a - a_hi.astype(F32)).astype(BF16)
    w = w_ref[0]
    w_hi = w.astype(BF16)
    w_lo = (w - w_hi.astype(F32)).astype(BF16)
    o_ref[0] = _dot(a_hi, w_hi) + _dot(a_lo, w_hi) + _dot(a_hi, w_lo) + b_ref[0]


def _modulation(c_all, w_mod, b_mod):
    depth, d, n = w_mod.shape
    rows = c_all.shape[0]
    tn = 1536
    return pl.pallas_call(
        _mod_kernel,
        name="adaln_mod",
        grid=(depth, n // tn),
        in_specs=[
            pl.BlockSpec((rows, d), lambda l, j: (0, 0)),
            pl.BlockSpec((1, d, tn), lambda l, j: (l, 0, j)),
            pl.BlockSpec((1, 1, tn), lambda l, j: (l, 0, j)),
        ],
        out_specs=pl.BlockSpec((1, rows, tn), lambda l, j: (l, 0, j)),
        out_shape=jax.ShapeDtypeStruct((depth, rows, n), F32),
        compiler_params=_cparams(("parallel", "parallel")),
    )(c_all, w_mod, b_mod.reshape(depth, 1, n))


def _h_specs(tm, d, n_lat_tiles, n_ctx_tiles, ctx_block0):
    return (pl.BlockSpec((tm, d), lambda i: (jnp.minimum(i, n_lat_tiles - 1), 0)),
            pl.BlockSpec((tm, d), lambda i: (ctx_block0 + jnp.clip(i - n_lat_tiles, 0, n_ctx_tiles - 1), 0)))


def _inproj_kernel(hl_ref, hc_ref, mod_ref, nw_ref, w_ref, o_ref, *, n_lat_tiles):
    h = jnp.where(pl.program_id(0) < n_lat_tiles, hl_ref[...], hc_ref[...])
    y = _rms(h, nw_ref[...])
    xn = (y * (1.0 + mod_ref[0, 1:2, :]) + mod_ref[0, 0:1, :]).astype(BF16)
    for c in range(0, Z_WIDTH, TN_IN):
        o_ref[:, c:c + TN_IN] = _dot(xn, w_ref[:, c:c + TN_IN]).astype(BF16)


def _layer_spec(arr, layer, **kw):
    tail = arr.shape[1:]
    return pl.BlockSpec((None,) + tail, lambda *_: (layer,) + (0,) * len(tail), **kw)


def _mod_spec(mod_all, layer, row_of_tile):
    return pl.BlockSpec((None, 1) + mod_all.shape[2:], lambda i: (layer, row_of_tile(i), 0, 0))


def _inproj(h_lat, h_ctx, ctx_block0, mod_all, norm_w, w_in_p, layer, batch):
    d = h_lat.shape[1]
    ntok = batch * (SEQ + CTX_LEN)
    tm = TM_TOK
    n_lat_tiles = batch * SEQ // tm
    n_ctx_tiles = batch * CTX_LEN // tm
    per_batch = SEQ // tm
    mod_row = lambda i: jnp.where(i < n_lat_tiles, i // per_batch, batch)
    return pl.pallas_call(
        functools.partial(_inproj_kernel, n_lat_tiles=n_lat_tiles),
        name="in_proj",
        grid=(ntok // tm,),
        in_specs=[
            *_h_specs(tm, d, n_lat_tiles, n_ctx_tiles, ctx_block0),
            _mod_spec(mod_all, layer, mod_row),
            _layer_spec(norm_w, layer),
            _layer_spec(w_in_p, layer, pipeline_mode=pl.Buffered(1)),
        ],
        out_specs=pl.BlockSpec((tm, Z_WIDTH), lambda i: (i, 0)),
        out_shape=jax.ShapeDtypeStruct((ntok, Z_WIDTH), BF16),
        compiler_params=_cparams(("parallel",)),
    )(h_lat, h_ctx, mod_all, norm_w, w_in_p)


def _mla_prep_kernel(aq_ref, akv_ref, akr_ref, qnw_ref, kvnw_ref, wq_ref, wkn_ref, wv_ref,
                     tqc_ref, tqs_ref, tkc_ref, tks_ref, q_out, k_out, v_out):
    qn = _rms(aq_ref[...].astype(F32), qnw_ref[...]).astype(BF16)
    qq = _dot(qn, wq_ref[...])
    tqc = tqc_ref[...]
    tqs = tqs_ref[...]
    for h in range(MLA_HEADS):
        blk = qq[:, h * LANE:(h + 1) * LANE]
        swapped = pltpu.roll(blk, LANE - MLA_ROPE, axis=1)
        q_out[:, h * LANE:(h + 1) * LANE] = (blk * tqc + swapped * tqs).astype(BF16)
    kvn = _rms(akv_ref[...].astype(F32), kvnw_ref[...]).astype(BF16)
    kn = _dot(kvn, wkn_ref[...])
    r = akr_ref[...].astype(F32)
    kr = r[:, :LANE] * tkc_ref[...] + r[:, LANE:] * tks_ref[...]
    for h in range(MLA_HEADS):
        lo = h * LANE
        k_out[:, lo:lo + LANE] = (kn[:, lo:lo + LANE] + kr).astype(BF16)
    v_out[...] = _dot(kvn, wv_ref[...]).astype(BF16)


def _mla_prep(z, qnw, kvnw, wq, wkn, wv, tabs, layer, n_lat_tiles):
    ntok = z.shape[0]
    tm = TM_IN
    per_batch = SEQ // tm
    tab_idx = lambda i: (jnp.where(i < n_lat_tiles, i % per_batch, per_batch), 0)
    hw = MLA_HEADS * LANE
    tab_spec = pl.BlockSpec((tm, LANE), tab_idx)
    return pl.pallas_call(
        _mla_prep_kernel,
        name="mla_prep",
        grid=(ntok // tm,),
        in_specs=[
            pl.BlockSpec((tm, MLA_Q_RANK), lambda i: (i, Z_AQ // MLA_Q_RANK)),
            pl.BlockSpec((tm, MLA_KV_RANK), lambda i: (i, Z_AKV // MLA_KV_RANK)),
            pl.BlockSpec((tm, 2 * LANE), lambda i: (i, Z_AKR // (2 * LANE))),
            _layer_spec(qnw, layer), _layer_spec(kvnw, layer),
            _layer_spec(wq, layer), _layer_spec(wkn, layer), _layer_spec(wv, layer),
            tab_spec, tab_spec, tab_spec, tab_spec,
        ],
        out_specs=[
            pl.BlockSpec((tm, hw), lambda i: (i, 0)),
            pl.BlockSpec((tm, hw), lambda i: (i, 0)),
            pl.BlockSpec((tm, MLA_HEADS * MLA_V), lambda i: (i, 0)),
        ],
        out_shape=[
            jax.ShapeDtypeStruct((ntok, hw), BF16),
            jax.ShapeDtypeStruct((ntok, hw), BF16),
            jax.ShapeDtypeStruct((ntok, MLA_HEADS * MLA_V), BF16),
        ],
        compiler_params=_cparams(("parallel",)),
    )(z, z, z, qnw, kvnw, wq, wkn, wv, *tabs)


def _attend_t(qs, n_chunks, chunk):
    if len(qs) > HEAD_GROUP:
        outs = []
        for g in range(0, len(qs), HEAD_GROUP):
            outs += _attend_t(qs[g:g + HEAD_GROUP], n_chunks, lambda h, c, g=g: chunk(g + h, c))
        return outs
    n_heads = len(qs)
    dv = chunk(0, 0)[1].shape[0]
    n_streams = min(KEY_STREAMS, n_chunks)
    streams = [(h, list(range(j, n_chunks, n_streams))) for j in range(n_streams) for h in range(n_heads)]
    n_steps = max(len(cs) for _, cs in streams)
    m = [None] * len(streams)
    o = [None] * len(streams)

    def scores(step):
        out = {}
        for i, (h, cs) in enumerate(streams):
            if step < len(cs):
                k, vt, bias = chunk(h, cs[step])
                s = _dot_nt(k, qs[h])
                out[i] = (vt, s if bias is None else s + bias)
        return out

    ahead = [scores(t) for t in range(min(SCORE_LOOKAHEAD, n_steps))]
    for t in range(n_steps):
        cur = ahead.pop(0)
        if t + SCORE_LOOKAHEAD < n_steps:
            ahead.append(scores(t + SCORE_LOOKAHEAD))
        live = list(cur)
        mc = {i: jnp.max(cur[i][1], axis=0, keepdims=True) for i in live}
        ones = jnp.ones((BF16_SUBLANES, cur[live[0]][0].shape[1]), BF16)
        vt1 = {i: jnp.concatenate([cur[i][0], ones], axis=0) for i in live}
        m_new = {i: mc[i] if m[i] is None else jnp.maximum(m[i], mc[i]) for i in live}
        alpha = {i: None if m[i] is None else jnp.exp2(m[i] - m_new[i]) for i in live}
        pv = {i: _dot(vt1[i], jnp.exp2(cur[i][1] - m_new[i]).astype(BF16)) for i in live}
        for i in live:
            o[i] = pv[i] if o[i] is None else o[i] * alpha[i] + pv[i]
            m[i] = m_new[i]

    outs = []
    for h in range(n_heads):
        mine = [j * n_heads + h for j in range(n_streams)]
        m_all = functools.reduce(jnp.maximum, [m[i] for i in mine])
        o_all = sum(o[i] * jnp.exp2(m[i] - m_all) for i in mine)
        outs.append(o_all[:dv] / o_all[dv:dv + 1])
    return outs


def _transpose_rows(src_ref, dst_ref, rows):
    for r in range(0, rows, TQ):
        dst_ref[:, r:r + TQ] = src_ref[r:r + TQ, :].astype(F32).T.astype(BF16)


def _mla_attn_kernel(q_ref, kl_ref, kc_ref, vl_ref, vc_ref, o_ref, vlt_s, vct_s, *, n_lat_q):
    t = pl.program_id(1)

    @pl.when(t == 0)
    def _():
        _transpose_rows(vl_ref, vlt_s, SEQ)
        _transpose_rows(vc_ref, vct_s, CTX_LEN)

    def attend(with_lat):
        def chunk(h, c):
            cols = slice(h * LANE, (h + 1) * LANE)
            rows = slice(h * MLA_V, (h + 1) * MLA_V)
            if c == 0:
                return kc_ref[:, cols], vct_s[rows, :], None
            keys = slice((c - 1) * MLA_KEY_CHUNK, c * MLA_KEY_CHUNK)
            return kl_ref[keys, cols], vlt_s[rows, keys], None

        qs = [q_ref[:, h * LANE:(h + 1) * LANE] for h in range(MLA_HEADS)]
        outs = _attend_t(qs, 1 + (SEQ // MLA_KEY_CHUNK if with_lat else 0), chunk)
        o_ref[...] = jnp.concatenate(outs, axis=0).T.astype(BF16)

    pl.when(t < n_lat_q)(lambda: attend(True))
    pl.when(t >= n_lat_q)(lambda: attend(False))


def _q_row_block(n_lat_q, ctx_base):
    return lambda b, t: jnp.where(t < n_lat_q, b * n_lat_q + t, ctx_base + b)


def _mla_attn(q, k, v, batch, need_ctx):
    ntok = q.shape[0]
    n_lat_q = SEQ // TQ
    ctx_base = batch * SEQ // CTX_LEN
    qrow = _q_row_block(n_lat_q, ctx_base)
    hw = MLA_HEADS * LANE
    vw = MLA_HEADS * MLA_V
    return pl.pallas_call(
        functools.partial(_mla_attn_kernel, n_lat_q=n_lat_q),
        name="mla_attn",
        grid=(batch, n_lat_q + (1 if need_ctx else 0)),
        in_specs=[
            pl.BlockSpec((TQ, hw), lambda b, t: (qrow(b, t), 0)),
            pl.BlockSpec((SEQ, hw), lambda b, t: (b, 0)),
            pl.BlockSpec((CTX_LEN, hw), lambda b, t: (ctx_base + b, 0)),
            pl.BlockSpec((SEQ, vw), lambda b, t: (b, 0)),
            pl.BlockSpec((CTX_LEN, vw), lambda b, t: (ctx_base + b, 0)),
        ],
        out_specs=pl.BlockSpec((TQ, vw), lambda b, t: (qrow(b, t), 0)),
        out_shape=jax.ShapeDtypeStruct((ntok, vw), BF16),
        scratch_shapes=[pltpu.VMEM((vw, SEQ), BF16), pltpu.VMEM((vw, CTX_LEN), BF16)],
        compiler_params=_cparams(("parallel", "arbitrary")),
    )(q, k, k, v, v)


def _gla_kernel(ql_ref, qc_ref, kl_ref, kc_ref, vl_ref, vc_ref, gl_ref, gc_ref, lrl_ref, lrc_ref,
                wg_ref, bg_ref, nw_ref, yl_ref, yc_ref,
                v_s, qd_s, ke_s, dec_s, o_s, st_s):
    n_tok = SEQ + CTX_LEN
    n_chunk = n_tok // GLA_CHUNK
    ctx_chunks = CTX_LEN // GLA_CHUNK
    ck = GLA_CHUNK
    tile = GLA_TILE
    per_tile = tile // ck
    qk_w = GLA_QK_W
    pair_k = GLA_PAIR * GLA_DK
    pair_v = GLA_PAIR * GLA_DV
    n_pairs = GLA_HEADS // GLA_PAIR

    v_s[0:SEQ, :] = vl_ref[...]
    v_s[SEQ:n_tok, :] = vc_ref[...]
    st_s[...] = jnp.zeros_like(st_s)

    row = lax.broadcasted_iota(jnp.int32, (tile, tile), 0)
    col = lax.broadcasted_iota(jnp.int32, (tile, tile), 1)
    same_chunk = (row // ck) == (col // ck)
    keep = (same_chunk & (col <= row), same_chunk & (col >= row))
    tri3 = tuple(jnp.concatenate([jnp.where(kp, 1.0, 0.0).astype(BF16)] * 3, axis=1) for kp in keep)
    q_head = lax.broadcasted_iota(jnp.int32, (tile, qk_w), 1) // GLA_DK
    bd_pair = (lax.broadcasted_iota(jnp.int32, (pair_k, pair_v), 0) // GLA_DK
               == lax.broadcasted_iota(jnp.int32, (pair_k, pair_v), 1) // GLA_DV)

    def log_decay(lr):
        zz = _dot(lr, wg_ref[...]) + bg_ref[...]
        return (jnp.minimum(zz, 0.0) - jnp.log1p(jnp.exp(-jnp.abs(zz)))) * (1.0 / GLA_GATE_TAU)

    def prologue(tiles):
        nt = range(len(tiles))
        dirs = range(2)
        heads = range(GLA_HEADS)
        rows = [pl.ds(tiles[t][4], tile) for t in nt]
        q = [tiles[t][0].astype(F32) * (GLA_DK ** -0.5) for t in nt]
        k = [tiles[t][1].astype(F32) for t in nt]
        la = [log_decay(tiles[t][3]) for t in nt]
        cum = [[_dot(tri3[d], jnp.concatenate(_split3(la[t][:, d * qk_w:(d + 1) * qk_w]), axis=0))
                for d in dirs] for t in nt]
        edge = (ck - 1, 0)
        lasts = [[[cum[t][d][c * ck + edge[d]:c * ck + edge[d] + 1, :] for c in range(per_tile)]
                  for d in dirs] for t in nt]
        last = [[jnp.concatenate([jnp.broadcast_to(x, (ck, qk_w)) for x in lasts[t][d]], axis=0)
                 for d in dirs] for t in nt]
        q_dec = [[(q[t] * jnp.exp(cum[t][d])).astype(BF16) for d in dirs] for t in nt]
        k_inv = [[(k[t] * jnp.exp(-cum[t][d])).astype(BF16) for d in dirs] for t in nt]
        for t in nt:
            for d in dirs:
                qd_s[d, rows[t], :] = q_dec[t][d]
                ke_s[d, rows[t], :] = (k[t] * jnp.exp(last[t][d] - cum[t][d])).astype(BF16)
                for c in range(per_tile):
                    dec_s[d, tiles[t][4] // ck + c] = jnp.broadcast_to(jnp.exp(lasts[t][d][c]), (8, qk_w))
        qm = [[[jnp.where(q_head == h, q_dec[t][d], jnp.zeros((), BF16)) for h in heads] for d in dirs] for t in nt]
        s = [[[_dot_nt(qm[t][d][h], k_inv[t][d]) for h in heads] for d in dirs] for t in nt]
        att = [[[jnp.where(keep[d], s[t][d][h], 0.0).astype(BF16) for h in heads] for d in dirs] for t in nt]
        o = [[[_dot(att[t][d][h], tiles[t][2][:, h * GLA_DV:(h + 1) * GLA_DV]) for h in heads]
              for d in dirs] for t in nt]
        for t in nt:
            for d in dirs:
                for h in heads:
                    o_s[d, rows[t], h * GLA_DV:(h + 1) * GLA_DV] = o[t][d][h]

    def lat_tiles(i, carry):
        tiles = []
        for j in range(GLA_TILES_PER_STEP):
            r0 = pl.multiple_of((i * GLA_TILES_PER_STEP + j) * tile, tile)
            rows = pl.ds(r0, tile)
            tiles.append((ql_ref[rows, :], kl_ref[rows, :], vl_ref[rows, :], lrl_ref[rows, :], r0))
        prologue(tiles)
        return carry

    lax.fori_loop(0, SEQ // (tile * GLA_TILES_PER_STEP), lat_tiles, 0)
    for r0 in range(0, CTX_LEN, tile):
        rows = slice(r0, r0 + tile)
        prologue([(qc_ref[rows, :], kc_ref[rows, :], vc_ref[rows, :], lrc_ref[rows, :], SEQ + r0)])

    def scan_step(n, d):
        rows = pl.ds(pl.multiple_of(n * ck, ck), ck)
        dec = dec_s[d, n]
        for p in range(n_pairs):
            kcols = slice(p * pair_k, (p + 1) * pair_k)
            vcols = slice(p * pair_v, (p + 1) * pair_v)
            st = st_s[d, p]
            o_s[d, rows, vcols] = o_s[d, rows, vcols] + _dot(qd_s[d, rows, kcols], st.astype(BF16))
            kv = _dot_tn(ke_s[d, rows, kcols], v_s[rows, vcols])
            decay_rows = jnp.broadcast_to(dec[0:1, kcols], (pair_v, pair_k)).T
            st_s[d, p] = st * decay_rows + jnp.where(bd_pair, kv, 0.0)

    def body(i, carry):
        nf = jnp.where(i < ctx_chunks, n_chunk - ctx_chunks + i, i - ctx_chunks)
        nb = n_chunk - 1 - i
        scan_step(nf, 0)
        scan_step(nb, 1)
        return carry

    lax.fori_loop(0, n_chunk, body, 0, unroll=2)

    nw = nw_ref[...]

    def finish(r0, rows, g):
        o = o_s[0, r0:r0 + rows, :] + o_s[1, r0:r0 + rows, :]
        g = g.astype(F32)
        parts = []
        for h in range(GLA_HEADS):
            parts.append(_rms(o[:, h * GLA_DV:(h + 1) * GLA_DV], nw))
        y = jnp.concatenate(parts, axis=1)
        return (y * (g * _sigmoid(g))).astype(BF16)

    rt = GLA_TILE
    for i in range(SEQ // rt):
        yl_ref[i * rt:(i + 1) * rt, :] = finish(i * rt, rt, gl_ref[i * rt:(i + 1) * rt, :])
    yc_ref[...] = finish(SEQ, CTX_LEN, gc_ref[...])


def _gla(z, wg, bg, nw, layer, batch):
    ctx_base = batch * SEQ // CTX_LEN
    n_tok = SEQ + CTX_LEN

    def lat(width, col0):
        return pl.BlockSpec((SEQ, width), lambda b: (b, col0 // width))

    def ctx(width, col0):
        return pl.BlockSpec((CTX_LEN, width), lambda b: (ctx_base + b, col0 // width))

    return pl.pallas_call(
        _gla_kernel,
        name="gla_scan",
        grid=(batch,),
        in_specs=[
            lat(GLA_QK_W, Z_BQ), ctx(GLA_QK_W, Z_BQ),
            lat(GLA_QK_W, Z_BK), ctx(GLA_QK_W, Z_BK),
            lat(GLA_V_W, Z_BV), ctx(GLA_V_W, Z_BV),
            lat(GLA_V_W, Z_BG), ctx(GLA_V_W, Z_BG),
            lat(LANE, Z_LR), ctx(LANE, Z_LR),
            _layer_spec(wg, layer), _layer_spec(bg, layer), _layer_spec(nw, layer),
        ],
        out_specs=[
            pl.BlockSpec((SEQ, GLA_V_W), lambda b: (b, 0)),
            pl.BlockSpec((CTX_LEN, GLA_V_W), lambda b: (b, 0)),
        ],
        out_shape=[
            jax.ShapeDtypeStruct((batch * SEQ, GLA_V_W), BF16),
            jax.ShapeDtypeStruct((batch * CTX_LEN, GLA_V_W), BF16),
        ],
        scratch_shapes=[
            pltpu.VMEM((n_tok, GLA_V_W), BF16),
            pltpu.VMEM((2, n_tok, GLA_QK_W), BF16),
            pltpu.VMEM((2, n_tok, GLA_QK_W), BF16),
            pltpu.VMEM((2, n_tok // GLA_CHUNK, 8, GLA_QK_W), F32),
            pltpu.VMEM((2, n_tok, GLA_V_W), F32),
            pltpu.VMEM((2, GLA_HEADS // GLA_PAIR, GLA_PAIR * GLA_DK, GLA_PAIR * GLA_DV), F32),
        ],
        compiler_params=_cparams(("parallel",)),
    )(z, z, z, z, z, z, z, z, z, z, wg, bg, nw)


def _na_kernel(q_ref, kl_ref, kc_ref, vl_ref, vc_ref, bias_ref, o_ref, vlt_s, vct_s, *, n_lat_q):
    lane = lax.broadcasted_iota(jnp.int32, (TQ, LANE), 1)
    t = pl.program_id(1)

    @pl.when(t == 0)
    def _():
        _transpose_rows(vl_ref, vlt_s, SEQ)
        _transpose_rows(vc_ref, vct_s, CTX_LEN)

    def attend(with_band):
        if with_band:
            r0 = jnp.clip(NA_QROWS * t - NA_KH // 2, 0, GRID_ROWS - NA_BAND_ROWS)
            start = pl.multiple_of(r0 * GRID_W, NA_QBLK)
            kind = jnp.where(t == 0, 0, jnp.where(t == n_lat_q - 1, 2, 1))

        def chunk(h, c):
            cols = slice((h // 2) * LANE, (h // 2 + 1) * LANE)
            rows = slice(h * NA_HEAD_DIM, (h + 1) * NA_HEAD_DIM)
            if c == 0:
                return kc_ref[:, cols], vct_s[rows, :], None
            off = (c - 1) * KEY_CHUNK
            keys = pl.ds(pl.multiple_of(start + off, KEY_CHUNK), KEY_CHUNK)
            bias_rows = pl.ds(pl.multiple_of(kind * NA_BAND + off, KEY_CHUNK), KEY_CHUNK)
            return kl_ref[keys, cols], vlt_s[rows, keys], bias_ref[h, bias_rows, :]

        qs = []
        for h in range(NA_HEADS):
            head_lanes = lane < NA_HEAD_DIM if h % 2 == 0 else lane >= NA_HEAD_DIM
            qs.append(jnp.where(head_lanes, q_ref[:, (h // 2) * LANE:(h // 2 + 1) * LANE], jnp.zeros((), BF16)))
        outs = _attend_t(qs, 1 + (NA_BAND // KEY_CHUNK if with_band else 0), chunk)
        o_ref[...] = jnp.concatenate(outs, axis=0).T.astype(BF16)

    pl.when(t < n_lat_q)(lambda: attend(True))
    pl.when(t >= n_lat_q)(lambda: attend(False))


def _na(z, bias, layer, batch, need_ctx):
    ntok = z.shape[0]
    n_lat_q = SEQ // TQ
    ctx_base = batch * SEQ // CTX_LEN
    qrow = _q_row_block(n_lat_q, ctx_base)
    w = NA_W
    return pl.pallas_call(
        functools.partial(_na_kernel, n_lat_q=n_lat_q),
        name="na_attn",
        grid=(batch, n_lat_q + (1 if need_ctx else 0)),
        in_specs=[
            pl.BlockSpec((TQ, w), lambda b, t: (qrow(b, t), Z_CQ // w)),
            pl.BlockSpec((SEQ, w), lambda b, t: (b, Z_CK // w)),
            pl.BlockSpec((CTX_LEN, w), lambda b, t: (ctx_base + b, Z_CK // w)),
            pl.BlockSpec((SEQ, w), lambda b, t: (b, Z_CV // w)),
            pl.BlockSpec((CTX_LEN, w), lambda b, t: (ctx_base + b, Z_CV // w)),
            _layer_spec(bias, layer, pipeline_mode=pl.Buffered(1)),
        ],
        out_specs=pl.BlockSpec((TQ, w), lambda b, t: (qrow(b, t), 0)),
        out_shape=jax.ShapeDtypeStruct((ntok, w), BF16),
        scratch_shapes=[pltpu.VMEM((w, SEQ), BF16), pltpu.VMEM((w, CTX_LEN), BF16)],
        compiler_params=_cparams(("parallel", "arbitrary")),
    )(z, z, z, z, z, bias)


def _merge_kernel(hl_ref, hc_ref, ya_ref, ybl_ref, ybc_ref, yc_ref, ga_ref, gb_ref, gc_ref, mod_ref,
                  wa_ref, wb_ref, wc_ref, wo_ref, o_ref, *, n_lat_tiles):
    is_lat = pl.program_id(0) < n_lat_tiles
    h = jnp.where(is_lat, hl_ref[...], hc_ref[...])
    yb = jnp.where(is_lat, ybl_ref[...], ybc_ref[...])
    m = (_sigmoid(ga_ref[...].astype(F32)) * _dot(ya_ref[...], wa_ref[...])
         + _sigmoid(gb_ref[...].astype(F32)) * _dot(yb, wb_ref[...])
         + _sigmoid(gc_ref[...].astype(F32)) * _dot(yc_ref[...], wc_ref[...]))
    o_ref[...] = h + mod_ref[0, 2:3, :] * _dot(m.astype(BF16), wo_ref[...])


def _merge(h_lat, h_ctx, ctx_block0, ya, yb_lat, yb_ctx, yc, z, mod_all, wa, wb, wc, wo, layer, batch, need_ctx):
    d = h_lat.shape[1]
    ntok = batch * (SEQ + CTX_LEN)
    tm = TM_TOK
    n_lat_tiles = batch * SEQ // tm
    n_tiles = ntok // tm if need_ctx else n_lat_tiles
    per_batch = SEQ // tm
    bw = wa.shape[1]
    tok = lambda width: pl.BlockSpec((tm, width), lambda i: (i, 0))
    gate = lambda g: pl.BlockSpec((tm, d), lambda i: (i, Z_GATES // d + g))
    return pl.pallas_call(
        functools.partial(_merge_kernel, n_lat_tiles=n_lat_tiles),
        name="merge",
        grid=(n_tiles,),
        in_specs=[
            *_h_specs(tm, d, n_lat_tiles, batch * CTX_LEN // tm, ctx_block0), tok(bw),
            pl.BlockSpec((tm, bw), lambda i: (jnp.minimum(i, n_lat_tiles - 1), 0)),
            pl.BlockSpec((tm, bw), lambda i: (jnp.maximum(i - n_lat_tiles, 0), 0)),
            tok(bw), gate(0), gate(1), gate(2),
            _mod_spec(mod_all, layer, lambda i: jnp.where(i < n_lat_tiles, i // per_batch, batch)),
            _layer_spec(wa, layer), _layer_spec(wb, layer), _layer_spec(wc, layer), _layer_spec(wo, layer),
        ],
        out_specs=tok(d),
        out_shape=jax.ShapeDtypeStruct((ntok, d), F32),
        compiler_params=_cparams(("parallel",)),
    )(h_lat, h_ctx, ya, yb_lat, yb_ctx, yc, z, z, z, mod_all, wa, wb, wc, wo)


def _ffn_kernel(h_ref, mod_ref, nw_ref, win_ref, wout_ref, fw_ref, o_ref, *, final):
    h = h_ref[...]
    u = (_rms(h, nw_ref[...]) * (1.0 + mod_ref[0, 4:5, :]) + mod_ref[0, 3:4, :]).astype(BF16)
    acc = None
    for lo, hi in FFN_CHUNKS:
        g = _dot(u, win_ref[:, lo:hi])
        up = _dot(u, win_ref[:, FFN_HIDDEN + lo:FFN_HIDDEN + hi])
        act = (g * _sigmoid(g) * up).astype(BF16)
        part = _dot(act, wout_ref[lo:hi, :])
        acc = part if acc is None else acc + part
    out = h + mod_ref[0, 5:6, :] * acc
    if final:
        out = _rms(out, fw_ref[...])
    o_ref[...] = out


def _ffn(h, mod_all, norm_w, w_in, w_out, final_w, layer, batch, need_ctx, final):
    ntok, d = h.shape
    tm = TM_TOK
    n_lat_tiles = batch * SEQ // tm
    n_tiles = ntok // tm if need_ctx else n_lat_tiles
    per_batch = SEQ // tm
    resident = dict(pipeline_mode=pl.Buffered(1))
    out_rows = n_tiles * tm if final else ntok
    return pl.pallas_call(
        functools.partial(_ffn_kernel, final=final),
        name="ffn",
        grid=(n_tiles,),
        in_specs=[
            pl.BlockSpec((tm, d), lambda i: (i, 0)),
            _mod_spec(mod_all, layer, lambda i: jnp.where(i < n_lat_tiles, i // per_batch, batch)),
            _layer_spec(norm_w, layer),
            _layer_spec(w_in, layer, **resident),
            _layer_spec(w_out, layer, **resident),
            pl.BlockSpec((1, d), lambda i: (0, 0)),
        ],
        out_specs=pl.BlockSpec((tm, d), lambda i: (i, 0)),
        out_shape=jax.ShapeDtypeStruct((out_rows, d), F32),
        compiler_params=_cparams(("parallel",)),
    )(h, mod_all, norm_w, w_in, w_out, final_w)


def _pad_cols(w, width):
    return jnp.pad(w, ((0, 0), (0, 0), (0, width - w.shape[-1])))


def _prep_w_in(w_in):
    a0 = 0
    b0 = MLA_Q_RANK + MLA_KV_RANK + MLA_ROPE
    c0 = b0 + 2 * GLA_QK_W + 2 * GLA_V_W + 2 * GLA_GATE_RANK
    g0 = c0 + 3 * NA_W
    sl = lambda lo, n: w_in[:, :, lo:lo + n]
    aq = sl(a0, MLA_Q_RANK)
    akv = sl(a0 + MLA_Q_RANK, MLA_KV_RANK)
    kr0 = a0 + MLA_Q_RANK + MLA_KV_RANK
    half = MLA_ROPE // 2
    kr = sl(kr0, MLA_ROPE)
    kr_sw = jnp.concatenate([sl(kr0 + half, half), sl(kr0, half)], axis=-1)
    zeros = lambda n: jnp.zeros(w_in.shape[:2] + (n,), w_in.dtype)
    akr = jnp.concatenate([zeros(MLA_NOPE), kr, zeros(LANE - MLA_NOPE - MLA_ROPE),
                           zeros(MLA_NOPE), kr_sw, zeros(LANE - MLA_NOPE - MLA_ROPE)], axis=-1)
    bq = sl(b0, GLA_QK_W)
    bk = sl(b0 + GLA_QK_W, GLA_QK_W)
    bv = sl(b0 + 2 * GLA_QK_W, GLA_V_W)
    bg = sl(b0 + 2 * GLA_QK_W + GLA_V_W, GLA_V_W)
    lr = _pad_cols(sl(b0 + 2 * GLA_QK_W + 2 * GLA_V_W, 2 * GLA_GATE_RANK), LANE)
    cq = sl(c0, NA_W) * (NA_SCALE * LOG2E)
    ck = sl(c0 + NA_W, NA_W)
    cv = sl(c0 + 2 * NA_W, NA_W)
    gates = sl(g0, 3 * D_MODEL)
    pieces = [aq, lr, cq, ck, cv, bv, bg, bq, bk, akv, akr, gates]
    out = jnp.concatenate([p.astype(BF16) for p in pieces], axis=-1)
    assert out.shape[-1] == Z_WIDTH
    return out


def _prep_mla_weights(w_q_up, w_kv_up):
    depth = w_q_up.shape[0]
    hd = MLA_NOPE + MLA_ROPE
    half = MLA_ROPE // 2
    wq = w_q_up.reshape(depth, MLA_Q_RANK, MLA_HEADS, hd)
    assert LANE - hd == MLA_ROPE
    wq_p = jnp.concatenate([wq, wq[..., MLA_NOPE + half:], wq[..., MLA_NOPE:MLA_NOPE + half]],
                           axis=-1).reshape(depth, MLA_Q_RANK, -1)
    wkv = w_kv_up.reshape(depth, MLA_KV_RANK, MLA_HEADS, MLA_NOPE + MLA_V)
    wkn = jnp.concatenate([wkv[..., :MLA_NOPE],
                           jnp.zeros((depth, MLA_KV_RANK, MLA_HEADS, LANE - MLA_NOPE), w_kv_up.dtype)], axis=-1)
    wv = wkv[..., MLA_NOPE:]
    return (wq_p.astype(BF16), wkn.reshape(depth, MLA_KV_RANK, -1).astype(BF16),
            wv.reshape(depth, MLA_KV_RANK, -1).astype(BF16))


def _rope_tables():
    t = np.arange(SEQ)
    rows = (t // GRID_W).astype(np.float32)
    cols = (t % GRID_W).astype(np.float32)
    n_freq = MLA_ROPE // 4
    inv_freq = jnp.asarray(ROPE_BASE, F32) ** (-jnp.arange(n_freq, dtype=F32) / n_freq)
    ang = jnp.concatenate([jnp.asarray(rows)[:, None] * inv_freq, jnp.asarray(cols)[:, None] * inv_freq], axis=-1)
    cos, sin = jnp.cos(ang), jnp.sin(ang)
    cos = jnp.concatenate([cos, jnp.ones((TM_IN, MLA_ROPE // 2), F32)], axis=0)
    sin = jnp.concatenate([sin, jnp.zeros((TM_IN, MLA_ROPE // 2), F32)], axis=0)
    n = cos.shape[0]
    tail = jnp.zeros((n, LANE - MLA_NOPE - MLA_ROPE), F32)
    c_rot = jnp.concatenate([cos, cos], axis=-1)
    s_rot = jnp.concatenate([-sin, sin], axis=-1)
    nope1 = jnp.ones((n, MLA_NOPE), F32)
    nope0 = jnp.zeros((n, MLA_NOPE), F32)
    tqc = jnp.concatenate([nope1, c_rot, tail], axis=-1) * (MLA_SCALE * LOG2E)
    tqs = jnp.concatenate([nope0, s_rot, tail], axis=-1) * (MLA_SCALE * LOG2E)
    tkc = jnp.concatenate([nope0, c_rot, tail], axis=-1)
    tks = jnp.concatenate([nope0, s_rot, tail], axis=-1)
    return tqc, tqs, tkc, tks


def _na_bias_tables(rpb):
    depth = rpb.shape[0]
    w2 = 2 * GRID_W
    lo = GRID_W - NA_KW
    u = jnp.pad(rpb.astype(F32), ((0, 0), (0, 0), (0, 0), (lo, w2 - lo - (2 * NA_KW - 1))))
    toep = jnp.stack([u[..., GRID_W - 1 - qc:w2 - 1 - qc] for qc in range(GRID_W)], axis=-2)
    col = np.arange(GRID_W)
    cs = np.clip(col - NA_KW // 2, 0, GRID_W - NA_KW)
    col_ok = (col[None, :] >= cs[:, None]) & (col[None, :] < cs[:, None] + NA_KW)
    toep = jnp.swapaxes(jnp.where(jnp.asarray(col_ok), toep * LOG2E, -1e30), -1, -2)
    masked = jnp.full((depth, NA_HEADS, GRID_W, GRID_W), -1e30, F32)
    n_blocks = GRID_ROWS // NA_QROWS
    k_rows = []
    for g in (0, 1, n_blocks - 1):
        band0 = int(np.clip(NA_QROWS * g - NA_KH // 2, 0, GRID_ROWS - NA_BAND_ROWS))
        for i in range(NA_BAND_ROWS):
            kr = band0 + i
            blocks = []
            for j in range(NA_QROWS):
                qr = NA_QROWS * g + j
                r0 = int(np.clip(qr - NA_KH // 2, 0, GRID_ROWS - NA_KH))
                blocks.append(toep[:, :, kr - qr + NA_KH - 1] if r0 <= kr < r0 + NA_KH else masked)
            k_rows.append(jnp.concatenate(blocks, axis=-1))
    return jnp.concatenate(k_rows, axis=-2)


def _prep_gla_gate(w_f, b_f, w_b, b_b):
    depth = w_f.shape[0]
    wg = jnp.zeros((depth, LANE, 2 * GLA_QK_W), F32)
    wg = wg.at[:, :GLA_GATE_RANK, :GLA_QK_W].set(w_f)
    wg = wg.at[:, GLA_GATE_RANK:2 * GLA_GATE_RANK, GLA_QK_W:].set(w_b)
    bg = jnp.concatenate([b_f, b_b], axis=-1)[:, None, :]
    return wg.astype(BF16), bg


def kernel(x, c, ctx, c_ctx, w_mod, b_mod, norm1_w, w_in, mla_q_norm_w, mla_kv_norm_w, mla_w_q_up, mla_w_kv_up, gla_w_gate_f, gla_b_gate_f, gla_w_gate_b, gla_b_gate_b, gla_norm_w, na_rpb, w_a_o, w_b_o, w_c_o, w_out, norm2_w, w_ffn_in, w_ffn_out, final_norm_w):
    batch, seq, d = x.shape
    depth = w_mod.shape[0]
    assert seq == SEQ and d == D_MODEL and ctx.shape[1] == CTX_LEN
    assert (batch * CTX_LEN) % TM_IN == 0
    n_lat_in = batch * SEQ // TM_IN

    mod_rows = -(-(batch + 1) // 8) * 8
    c_all = jnp.concatenate([c, c_ctx[None, :], jnp.zeros((mod_rows - batch - 1, d), F32)], axis=0)
    mod_all = _modulation(c_all, w_mod, b_mod).reshape(depth, mod_rows, 6, d)

    w_in_p = _prep_w_in(w_in)
    wq_p, wkn_p, wv_p = _prep_mla_weights(mla_w_q_up, mla_w_kv_up)
    tabs = _rope_tables()
    na_bias = _na_bias_tables(na_rpb)
    wg_p, bg_p = _prep_gla_gate(gla_w_gate_f, gla_b_gate_f, gla_w_gate_b, gla_b_gate_b)
    wa, wb, wc, wo = (w.astype(BF16) for w in (w_a_o, w_b_o, w_c_o, w_out))
    wfi, wfo = w_ffn_in.astype(BF16), w_ffn_out.astype(BF16)
    n1, n2, qnw, kvnw, gnw = (w[:, None, :] for w in (norm1_w, norm2_w, mla_q_norm_w, mla_kv_norm_w, gla_norm_w))

    hs = (x.reshape(batch * SEQ, d), ctx.reshape(batch * CTX_LEN, d), 0)
    for i in range(depth):
        need_ctx = i < depth - 1
        z = _inproj(*hs, mod_all, n1, w_in_p, i, batch)
        q, k, v = _mla_prep(z, qnw, kvnw, wq_p, wkn_p, wv_p, tabs, i, n_lat_in)
        ya = _mla_attn(q, k, v, batch, need_ctx)
        yb_lat, yb_ctx = _gla(z, wg_p, bg_p, gnw, i, batch)
        yc = _na(z, na_bias, i, batch, need_ctx)
        h = _merge(*hs, ya, yb_lat, yb_ctx, yc, z, mod_all, wa, wb, wc, wo, i, batch, need_ctx)
        h = _ffn(h, mod_all, n2, wfi, wfo, final_norm_w[None, :], i, batch, need_ctx, not need_ctx)
        hs = (h, h, batch * SEQ // TM_TOK)
    return h.reshape(batch, SEQ, d)
```
